```python
import jax, jax.numpy as jnp
from jax import lax
import numpy as np

D_MODEL = 1024
BATCH = 2
SEQ = 16384
DEPTH = 2

EPS = 1e-6
SSM_EXPAND = 2
D_INNER = SSM_EXPAND * D_MODEL
SSM_HEAD_DIM = 64
N_SSM_HEADS = D_INNER // SSM_HEAD_DIM
N_SSM_GROUPS = 4
D_STATE = 128
SSM_CONV = 4
SSM_CHUNK = 128
CONV_DIM = D_INNER + 2 * N_SSM_GROUPS * D_STATE
D_SC = D_MODEL
SC_CONV = 3
N_ATT_HEADS = 16
ATT_HEAD_DIM = 64
ATT_V_DIM = 64
Q_LORA = 384
KV_LORA = 256
N_IDX_HEADS = 8
D_IDX = 64
TOPK_MAX = 256
Q_BLOCK = 128
ATT_SCALE = ATT_HEAD_DIM ** -0.5
IDX_SCALE = (N_IDX_HEADS * D_IDX) ** -0.5
N_EXPERT_GROUPS = 4
EXPERTS_PER_GROUP = 4
N_EXPERTS = N_EXPERT_GROUPS * EXPERTS_PER_GROUP
D_EXPERT = 512
TOP_K_EXPERTS = 2
SPLIT_SIZES = (D_INNER, CONV_DIM, N_SSM_HEADS,
               D_SC, D_SC, D_SC,
               Q_LORA, KV_LORA, D_IDX, N_IDX_HEADS,
               D_MODEL, D_MODEL, D_MODEL)
SPLIT_POINTS = tuple(int(v) for v in np.cumsum(SPLIT_SIZES)[:-1])
N_IN = int(sum(SPLIT_SIZES))

kernel_name = "hybrid_ssd_shortconv_dsa_hmoe"

F32 = jnp.float32


def rms_norm(x, g):
    xf = x.astype(F32)
    y = xf * lax.rsqrt(jnp.mean(xf * xf, axis=-1, keepdims=True) + EPS)
    return (y * g.astype(F32)).astype(x.dtype)


def causal_depthwise_conv(x, w):
    width, c = w.shape
    return lax.conv_general_dilated(
        x, w[:, None, :].astype(x.dtype), window_strides=(1,), padding=[(width - 1, 0)],
        dimension_numbers=("NWC", "WIO", "NWC"), feature_group_count=c)


def ssd_chunked(xh, dt, a, bm, cm):
    b, s, h, p = xh.shape
    g, n = bm.shape[2], bm.shape[3]
    r = h // g
    nc = s // SSM_CHUNK
    q = SSM_CHUNK
    xdt = (xh * dt[..., None]).reshape(b, nc, q, g, r, p)
    da = (dt * a).reshape(b, nc, q, g, r)
    bc = bm.reshape(b, nc, q, g, n)
    cc = cm.reshape(b, nc, q, g, n)
    da_cs = jnp.cumsum(da, axis=2)
    causal = jnp.tril(jnp.ones((q, q), dtype=bool))[:, :, None, None]
    seg = da_cs[:, :, :, None] - da_cs[:, :, None, :]
    decay = jnp.exp(jnp.where(causal, seg, -jnp.inf))
    cb = jnp.einsum("bclgn,bcsgn->bclsg", cc, bc)
    y_diag = jnp.einsum("bclsgr,bcsgrp->bclgrp", cb[..., None] * decay, xdt)
    decay_to_end = jnp.exp(da_cs[:, :, -1:] - da_cs)
    chunk_states = jnp.einsum("bclgn,bclgrp->bcgrpn", bc, xdt * decay_to_end[..., None])
    chunk_decay = jnp.exp(da_cs[:, :, -1])

    def carry_state(state, inp):
        new_c, dec_c = inp
        return dec_c[..., None, None] * state + new_c, state

    init = jnp.zeros((b, g, r, p, n), F32)
    _, entering = lax.scan(carry_state, init,
                           (jnp.moveaxis(chunk_states, 1, 0), jnp.moveaxis(chunk_decay, 1, 0)))
    entering = jnp.moveaxis(entering, 0, 1)
    y_off = jnp.einsum("bclgn,bcgrpn->bclgrp", cc, entering) * jnp.exp(da_cs)[..., None]
    return (y_diag + y_off).reshape(b, s, h, p)


def ssd_branch(z, xbc, dt_raw, conv_w, conv_b, dt_bias, a_log, d_skip, norm_g, w_out):
    b, s, _ = z.shape
    xbc = jax.nn.silu(causal_depthwise_conv(xbc, conv_w) + conv_b.astype(xbc.dtype))
    xs, bm, cm = jnp.split(xbc, [D_INNER, D_INNER + N_SSM_GROUPS * D_STATE], axis=-1)
    xh = xs.reshape(b, s, N_SSM_HEADS, SSM_HEAD_DIM).astype(F32)
    dt = jax.nn.softplus(dt_raw.astype(F32) + dt_bias.astype(F32))
    a = -jnp.exp(a_log.astype(F32))
    y = ssd_chunked(xh, dt, a,
                    bm.reshape(b, s, N_SSM_GROUPS, D_STATE).astype(F32),
                    cm.reshape(b, s, N_SSM_GROUPS, D_STATE).astype(F32))
    y = (y + d_skip.astype(F32)[:, None] * xh).reshape(b, s, D_INNER)
    yg = (y * jax.nn.silu(z.astype(F32))).reshape(b, s, N_SSM_GROUPS, D_INNER // N_SSM_GROUPS)
    yg = yg * lax.rsqrt(jnp.mean(yg * yg, axis=-1, keepdims=True) + EPS)
    yg = (yg.reshape(b, s, D_INNER) * norm_g.astype(F32)).astype(z.dtype)
    return yg @ w_out


def short_conv_branch(gate_b, gate_c, xin, conv_w, w_out):
    v = causal_depthwise_conv(gate_c * xin, conv_w)
    return (gate_b * v) @ w_out


def dsa_branch(c_q, c_kv, k_idx, w_idx, q_norm_g, kv_norm_g, k_norm_g,
               w_uq, w_qidx, w_uk, w_uv, w_out):
    b, s, _ = c_q.shape
    topk = min(TOPK_MAX, s // 4)
    c_q = rms_norm(c_q, q_norm_g)
    c_kv = rms_norm(c_kv, kv_norm_g)
    k_idx = rms_norm(k_idx, k_norm_g)
    key_pos = jnp.arange(s)

    def attend_block(blk):
        q0 = blk * Q_BLOCK
        cq = lax.dynamic_slice_in_dim(c_q, q0, Q_BLOCK, axis=1)
        wh = lax.dynamic_slice_in_dim(w_idx, q0, Q_BLOCK, axis=1).astype(F32) * IDX_SCALE
        q_pos = q0 + jnp.arange(Q_BLOCK)
        visible = key_pos[None, :] <= q_pos[:, None]
        qi = jnp.einsum("bqc,chd->bqhd", cq, w_qidx)
        dots = jnp.einsum("bqhd,bsd->bqhs", qi, k_idx).astype(F32)
        score = jnp.einsum("bqhs,bqh->bqs", jax.nn.relu(dots), wh)
        score = jnp.where(visible[None], score, -jnp.inf)
        _, sel = lax.top_k(score, topk)
        valid = sel <= q_pos[None, :, None]
        kv = jax.vmap(lambda lat, idx: lat[idx])(c_kv, sel)
        q = jnp.einsum("bqc,chd->bqhd", cq, w_uq)
        q_lat = jnp.einsum("bqhd,hdr->bqhr", q, w_uk)
        logits = jnp.einsum("bqhr,bqkr->bqhk", q_lat, kv).astype(F32) * ATT_SCALE
        logits = jnp.where(valid[:, :, None, :], logits, -jnp.inf)
        probs = jax.nn.softmax(logits, axis=-1).astype(kv.dtype)
        o_lat = jnp.einsum("bqhk,bqkr->bqhr", probs, kv)
        o = jnp.einsum("bqhr,hrv->bqhv", o_lat, w_uv)
        return o.reshape(b, Q_BLOCK, N_ATT_HEADS * ATT_V_DIM)

    out = lax.map(attend_block, jnp.arange(s // Q_BLOCK))
    out = jnp.moveaxis(out, 0, 1).reshape(b, s, N_ATT_HEADS * ATT_V_DIM)
    return out @ w_out


def hybrid_mixer(h, w_in, ssm_conv_w, ssm_conv_b, ssm_dt_bias, ssm_a_log, ssm_d, ssm_norm_g, ssm_w_out,
                 sc_conv_w, sc_w_out, att_q_norm_g, att_kv_norm_g, idx_k_norm_g, att_w_uq, idx_w_q,
                 att_w_uk, att_w_uv, att_w_out, mix_w_out):
    (z, xbc, dt_raw, sc_b, sc_c, sc_x, c_q, c_kv, k_idx, w_idx,
     g_a, g_b, g_c) = jnp.split(h @ w_in, SPLIT_POINTS, axis=-1)
    y_a = ssd_branch(z, xbc, dt_raw, ssm_conv_w, ssm_conv_b, ssm_dt_bias, ssm_a_log, ssm_d,
                     ssm_norm_g, ssm_w_out)
    y_b = short_conv_branch(sc_b, sc_c, sc_x, sc_conv_w, sc_w_out)
    y_c = dsa_branch(c_q, c_kv, k_idx, w_idx, att_q_norm_g, att_kv_norm_g, idx_k_norm_g,
                     att_w_uq, idx_w_q, att_w_uk, att_w_uv, att_w_out)
    merged = jax.nn.sigmoid(g_a) * y_a + jax.nn.sigmoid(g_b) * y_b + jax.nn.sigmoid(g_c) * y_c
    return merged @ mix_w_out


def hier_moe(h, w_group, b_group, w_expert, b_expert, w_gate, w_up, w_down):
    b, s, _ = h.shape
    group_probs = jax.nn.softmax((h @ w_group + b_group).astype(F32), axis=-1)
    p_group, g_sel = lax.top_k(group_probs, 1)
    exp_logits = (h @ w_expert + b_expert).astype(F32).reshape(b, s, N_EXPERT_GROUPS, EXPERTS_PER_GROUP)
    in_group = jnp.take_along_axis(exp_logits, g_sel[..., None], axis=2)[:, :, 0]
    p_in, e_sel = lax.top_k(jax.nn.softmax(in_group, axis=-1), TOP_K_EXPERTS)
    p_in = p_in / jnp.sum(p_in, axis=-1, keepdims=True)
    gate = p_group * p_in
    expert_id = g_sel * EXPERTS_PER_GROUP + e_sel
    combine = jnp.einsum("bsk,bske->bse", gate,
                         jax.nn.one_hot(expert_id, N_EXPERTS, dtype=F32)).astype(h.dtype)
    out = jnp.zeros_like(h)
    for e in range(N_EXPERTS):
        hid = jax.nn.silu(h @ w_gate[e]) * (h @ w_up[e])
        out = out + combine[..., e:e + 1] * (hid @ w_down[e])
    return out


def setup_inputs(seed: int = 0) -> dict:
    key = jax.random.key(seed)
    ks = jax.random.split(key, 30)

    def nrm(k, shape, scale):
        return jax.random.normal(k, shape, F32) * scale

    def gain(k, shape):
        return 1.0 + 0.05 * jax.random.normal(k, shape, F32)

    L = DEPTH
    dt0 = jnp.exp(jax.random.uniform(ks[5], (L, N_SSM_HEADS), F32) * (np.log(0.1) - np.log(0.001))
                  + np.log(0.001))
    return {
        "x": nrm(ks[0], (BATCH, SEQ, D_MODEL), 1.0),
        "norm1_g": gain(ks[1], (L, D_MODEL)),
        "w_in": nrm(ks[2], (L, D_MODEL, N_IN), D_MODEL ** -0.5),
        "ssm_conv_w": nrm(ks[3], (L, SSM_CONV, CONV_DIM), SSM_CONV ** -0.5),
        "ssm_conv_b": nrm(ks[4], (L, CONV_DIM), 0.02),
        "ssm_dt_bias": dt0 + jnp.log(-jnp.expm1(-dt0)),
        "ssm_a_log": jnp.log(jax.random.uniform(ks[6], (L, N_SSM_HEADS), F32, 1.0, 16.0)),
        "ssm_d": gain(ks[7], (L, N_SSM_HEADS)),
        "ssm_norm_g": gain(ks[8], (L, D_INNER)),
        "ssm_w_out": nrm(ks[9], (L, D_INNER, D_MODEL), D_INNER ** -0.5),
        "sc_conv_w": nrm(ks[10], (L, SC_CONV, D_SC), SC_CONV ** -0.5),
        "sc_w_out": nrm(ks[11], (L, D_SC, D_MODEL), D_SC ** -0.5),
        "att_q_norm_g": gain(ks[12], (L, Q_LORA)),
        "att_kv_norm_g": gain(ks[13], (L, KV_LORA)),
        "idx_k_norm_g": gain(ks[14], (L, D_IDX)),
        "att_w_uq": nrm(ks[15], (L, Q_LORA, N_ATT_HEADS, ATT_HEAD_DIM), Q_LORA ** -0.5),
        "idx_w_q": nrm(ks[16], (L, Q_LORA, N_IDX_HEADS, D_IDX), Q_LORA ** -0.5),
        "att_w_uk": nrm(ks[17], (L, N_ATT_HEADS, ATT_HEAD_DIM, KV_LORA), KV_LORA ** -0.5),
        "att_w_uv": nrm(ks[18], (L, N_ATT_HEADS, KV_LORA, ATT_V_DIM), KV_LORA ** -0.5),
        "att_w_out": nrm(ks[19], (L, N_ATT_HEADS * ATT_V_DIM, D_MODEL), (N_ATT_HEADS * ATT_V_DIM) ** -0.5),
        "mix_w_out": nrm(ks[20], (L, D_MODEL, D_MODEL), D_MODEL ** -0.5),
        "norm2_g": gain(ks[21], (L, D_MODEL)),
        "moe_w_group": nrm(ks[22], (L, D_MODEL, N_EXPERT_GROUPS), D_MODEL ** -0.5),
        "moe_b_group": nrm(ks[23], (L, N_EXPERT_GROUPS), 0.01),
        "moe_w_expert": nrm(ks[24], (L, D_MODEL, N_EXPERTS), D_MODEL ** -0.5),
        "moe_b_expert": nrm(ks[25], (L, N_EXPERTS), 0.01),
        "moe_w_gate": nrm(ks[26], (L, N_EXPERTS, D_MODEL, D_EXPERT), D_MODEL ** -0.5),
        "moe_w_up": nrm(ks[27], (L, N_EXPERTS, D_MODEL, D_EXPERT), D_MODEL ** -0.5),
        "moe_w_down": nrm(ks[28], (L, N_EXPERTS, D_EXPERT, D_MODEL), D_EXPERT ** -0.5),
        "final_norm_g": gain(ks[29], (D_MODEL,)),
    }


def reference(x, norm1_g, w_in, ssm_conv_w, ssm_conv_b, ssm_dt_bias, ssm_a_log, ssm_d, ssm_norm_g,
              ssm_w_out, sc_conv_w, sc_w_out, att_q_norm_g, att_kv_norm_g, idx_k_norm_g, att_w_uq,
              idx_w_q, att_w_uk, att_w_uv, att_w_out, mix_w_out, norm2_g, moe_w_group, moe_b_group,
              moe_w_expert, moe_b_expert, moe_w_gate, moe_w_up, moe_w_down, final_norm_g):
    for layer in range(DEPTH):
        h = rms_norm(x, norm1_g[layer])
        x = x + hybrid_mixer(h, w_in[layer], ssm_conv_w[layer], ssm_conv_b[layer], ssm_dt_bias[layer],
                             ssm_a_log[layer], ssm_d[layer], ssm_norm_g[layer], ssm_w_out[layer],
                             sc_conv_w[layer], sc_w_out[layer], att_q_norm_g[layer], att_kv_norm_g[layer],
                             idx_k_norm_g[layer], att_w_uq[layer], idx_w_q[layer], att_w_uk[layer],
                             att_w_uv[layer], att_w_out[layer], mix_w_out[layer])
        h = rms_norm(x, norm2_g[layer])
        x = x + hier_moe(h, moe_w_group[layer], moe_b_group[layer], moe_w_expert[layer],
                         moe_b_expert[layer], moe_w_gate[layer], moe_w_up[layer], moe_w_down[layer])
    return rms_norm(x, final_norm_g)
```

```python
import functools

import jax
import jax.numpy as jnp
from jax import lax
from jax.experimental import pallas as pl
from jax.experimental.pallas import tpu as pltpu

F32 = jnp.float32
BF16 = jnp.bfloat16
I32 = jnp.int32

EPS = 1e-6
LANES = 128
N_SSM_HEADS = 32
SSM_HEAD_DIM = 64
N_SSM_GROUPS = 4
D_STATE = 128
SSM_CONV = 4
SSM_CHUNK = 128
SC_CONV = 3
N_ATT_HEADS = 16
ATT_HEAD_DIM = 64
ATT_V_DIM = 64
Q_LORA = 384
KV_LORA = 256
N_IDX_HEADS = 8
D_IDX = 64
TOPK_MAX = 256
Q_BLOCK = 128
ATT_SCALE = ATT_HEAD_DIM ** -0.5
IDX_SCALE = (N_IDX_HEADS * D_IDX) ** -0.5
N_EXPERT_GROUPS = 4
EXPERTS_PER_GROUP = 4
N_EXPERTS = 16
D_EXPERT = 512

D_MODEL = 1024
D_INNER = 2048
COL_Z = 0
COL_XS = 2048
COL_BM = 4096
COL_CM = 4608
COL_SCB = 5120
COL_SCC = 6144
COL_SCX = 7168
COL_GA = 8192
COL_GB = 9216
COL_GC = 10240
COL_CKV = 11264
COL_CQ = 11520
COL_SMALL = 11904
N_PROJ = 12288
SM_DT = 0
SM_WIDX = 32
SM_KIDX = 64

PROJ_TM = 1024
PROJ_TN = 512
DSA_TK = 512
KEY_MIN = -2 ** 31
NEG_BIG = -1e30
VMEM_LIMIT = 60 * 1024 * 1024

_NT = (((1,), (1,)), ((), ()))


def _cparams(sem):
    return pltpu.CompilerParams(dimension_semantics=sem, vmem_limit_bytes=VMEM_LIMIT)


def _silu(v):
    return v * jax.nn.sigmoid(v)


def _softplus(v):
    return jnp.maximum(v, 0.0) + jnp.log(1.0 + jnp.exp(-jnp.abs(v)))


def _in_proj_kernel(x_ref, g_ref, w_ref, o_ref, small_ref, hn_ref, *, small_j, small_off):
    j = pl.program_id(1)

    @pl.when(j == 0)
    def _():
        x = x_ref[...]
        hn = x * lax.rsqrt(jnp.mean(x * x, axis=-1, keepdims=True) + EPS) * g_ref[...]
        hn_ref[...] = hn.astype(BF16)

    acc = jnp.dot(hn_ref[...], w_ref[...], preferred_element_type=F32)
    o_ref[...] = acc.astype(BF16)

    @pl.when(j == small_j)
    def _():
        small_ref[...] = acc[:, small_off:small_off + LANES]


def _in_proj(x2, g, w_perm):
    t, d = x2.shape
    tm = min(PROJ_TM, t)
    small_j, small_off = COL_SMALL // PROJ_TN, COL_SMALL % PROJ_TN
    return pl.pallas_call(
        functools.partial(_in_proj_kernel, small_j=small_j, small_off=small_off),
        grid=(t // tm, N_PROJ // PROJ_TN),
        in_specs=[pl.BlockSpec((tm, d), lambda i, j: (i, 0)),
                  pl.BlockSpec((1, d), lambda i, j: (0, 0)),
                  pl.BlockSpec((d, PROJ_TN), lambda i, j: (0, j))],
        out_specs=[pl.BlockSpec((tm, PROJ_TN), lambda i, j: (i, j)),
                   pl.BlockSpec((tm, LANES), lambda i, j: (i, 0))],
        out_shape=[jax.ShapeDtypeStruct((t, N_PROJ), BF16), jax.ShapeDtypeStruct((t, LANES), F32)],
        scratch_shapes=[pltpu.VMEM((tm, d), BF16)],
        compiler_params=_cparams(("parallel", "arbitrary")),
        name="in_proj",
    )(x2, g, w_perm)


def _split_dot(a, e_bf16):
    hi = a.astype(BF16)
    lo = (a - hi.astype(F32)).astype(BF16)
    return (jnp.dot(hi, e_bf16, preferred_element_type=F32) + jnp.dot(lo, e_bf16, preferred_element_type=F32))


def _ssd_kernel(z_ref, xs_ref, bm_ref, cm_ref, small_ref, dtt_ref, convw_ref, convb_ref, dtb_ref, dtbt_ref,
                alog_ref, alogt_ref, dskip_ref, ng_ref, e_ref, o_ref, buf_ref, xbc_ref, state_ref, y_ref):
    c = pl.program_id(1)
    L = SSM_CHUNK
    GW = D_INNER // N_SSM_GROUPS
    HALO = 8

    @pl.when(c == 0)
    def _():
        buf_ref[0:HALO, :] = jnp.zeros((HALO, buf_ref.shape[1]), F32)
        state_ref[...] = jnp.zeros(state_ref.shape, F32)

    buf_ref[HALO:HALO + L, 0:D_INNER] = xs_ref[...].astype(F32)
    buf_ref[HALO:HALO + L, D_INNER:D_INNER + GW] = bm_ref[...].astype(F32)
    buf_ref[HALO:HALO + L, D_INNER + GW:D_INNER + 2 * GW] = cm_ref[...].astype(F32)
    for cc in range(buf_ref.shape[1] // GW):
        cols = slice(cc * GW, (cc + 1) * GW)
        acc = jnp.broadcast_to(convb_ref[:, cols], (L, GW))
        for j in range(SSM_CONV):
            r0 = HALO - (SSM_CONV - 1) + j
            acc = acc + convw_ref[j:j + 1, cols] * buf_ref[r0:r0 + L, cols]
        xbc_ref[:, cols] = _silu(acc)
    buf_ref[0:HALO, :] = buf_ref[L:L + HALO, :]

    row = lax.broadcasted_iota(I32, (L, L), 0)
    col = lax.broadcasted_iota(I32, (L, L), 1)
    causal = row >= col
    tril = causal.astype(F32)
    triu = (row <= col).astype(F32)
    dt = _softplus(small_ref[...] + dtb_ref[...])
    da = dt * (-jnp.exp(alog_ref[...]))
    cs = jnp.dot(tril, da, precision=lax.Precision.HIGHEST, preferred_element_type=F32)
    dtt = _softplus(dtt_ref[0] + dtbt_ref[...])
    dat = dtt * (-jnp.exp(alogt_ref[...]))
    cst = jnp.dot(dat, triu, precision=lax.Precision.HIGHEST, preferred_element_type=F32)
    cs_last = cs[L - 1:L, :]
    e = e_ref[...]
    dt_e = _split_dot(dt, e)
    dte_e = _split_dot(jnp.exp(cs_last - cs), e)
    eo_e = _split_dot(jnp.exp(cs), e)
    cd_e = _split_dot(jnp.broadcast_to(jnp.exp(cs_last), (8, LANES)), e)[0:1, :]

    lane = lax.broadcasted_iota(I32, (L, LANES), 1)
    lo_half = lane < SSM_HEAD_DIM
    for g in range(N_SSM_GROUPS):
        gc = slice(g * GW, (g + 1) * GW)
        bm_g = xbc_ref[:, D_INNER + g * D_STATE:D_INNER + (g + 1) * D_STATE]
        cm_g = xbc_ref[:, D_INNER + GW + g * D_STATE:D_INNER + GW + (g + 1) * D_STATE]
        cm_b = cm_g.astype(BF16)
        cb = lax.dot_general(cm_b, bm_g.astype(BF16), _NT, preferred_element_type=F32)
        xs_g = xbc_ref[:, gc]
        xdt_g = xs_g * dt_e[:, gc]
        st = state_ref[g]
        y_off = jnp.dot(cm_b, st.astype(BF16), preferred_element_type=F32) * eo_e[:, gc]
        for pr in range(GW // LANES):
            x_pair = xdt_g[:, pr * LANES:(pr + 1) * LANES]
            y_pair = y_off[:, pr * LANES:(pr + 1) * LANES]
            for half in range(2):
                h = g * (GW // SSM_HEAD_DIM) + pr * 2 + half
                seg = cs[:, h:h + 1] - cst[h:h + 1, :]
                dec = jnp.exp(jnp.where(causal, seg, -jnp.inf))
                m = (cb * dec).astype(BF16)
                keep = lo_half if half == 0 else jnp.logical_not(lo_half)
                x_h = jnp.where(keep, x_pair, 0.0).astype(BF16)
                y_pair = y_pair + jnp.dot(m, x_h, preferred_element_type=F32)
            y_ref[:, g * GW + pr * LANES:g * GW + (pr + 1) * LANES] = y_pair
        xd = (xdt_g * dte_e[:, gc]).astype(BF16)
        new = jnp.dot(bm_g.T.astype(BF16), xd, preferred_element_type=F32)
        state_ref[g] = cd_e[:, gc] * st + new
        yg = (y_ref[:, gc] + dskip_ref[:, gc] * xs_g) * _silu(z_ref[:, gc].astype(F32))
        yg = yg * lax.rsqrt(jnp.mean(yg * yg, axis=-1, keepdims=True) + EPS)
        o_ref[:, gc] = (yg * ng_ref[:, gc]).astype(BF16)


def _ssd(proj, small, dtt, p, b, s):
    L = SSM_CHUNK
    nc = s // L
    conv_dim = D_INNER + 2 * N_SSM_GROUPS * D_STATE
    row = lambda bi, ci: bi * nc + ci
    full = lambda shape: pl.BlockSpec(shape, lambda bi, ci: tuple(0 for _ in shape))
    return pl.pallas_call(
        _ssd_kernel,
        grid=(b, nc),
        in_specs=[pl.BlockSpec((L, D_INNER), lambda bi, ci: (row(bi, ci), COL_Z // D_INNER)),
                  pl.BlockSpec((L, D_INNER), lambda bi, ci: (row(bi, ci), COL_XS // D_INNER)),
                  pl.BlockSpec((L, 512), lambda bi, ci: (row(bi, ci), COL_BM // 512)),
                  pl.BlockSpec((L, 512), lambda bi, ci: (row(bi, ci), COL_CM // 512)),
                  pl.BlockSpec((L, LANES), lambda bi, ci: (row(bi, ci), 0)),
                  pl.BlockSpec((1, N_SSM_HEADS, L), lambda bi, ci: (bi, 0, ci)),
                  full((SSM_CONV, conv_dim)), full((1, conv_dim)),
                  full((1, LANES)), full((N_SSM_HEADS, 1)), full((1, LANES)), full((N_SSM_HEADS, 1)),
                  full((1, D_INNER)), full((1, D_INNER)), full((LANES, D_INNER))],
        out_specs=pl.BlockSpec((L, D_INNER), lambda bi, ci: (row(bi, ci), 0)),
        out_shape=jax.ShapeDtypeStruct((b * s, D_INNER), BF16),
        scratch_shapes=[pltpu.VMEM((L + 8, conv_dim), F32), pltpu.VMEM((L, conv_dim), F32),
                        pltpu.VMEM((N_SSM_GROUPS, D_STATE, D_INNER // N_SSM_GROUPS), F32),
                        pltpu.VMEM((L, D_INNER), F32)],
        compiler_params=_cparams(("parallel", "arbitrary")),
        name="ssd",
    )(proj, proj, proj, proj, small, dtt, p["conv_w"], p["conv_b"], p["dt_b"], p["dt_bt"], p["a_log"], p["a_logt"],
      p["d_skip"], p["norm_g"], p["expand"])


def _sconv_kernel(b_ref, c_ref, x_ref, w_ref, o_ref, buf_ref):
    i = pl.program_id(1)
    tq = b_ref.shape[0]
    HALO = 8

    @pl.when(i == 0)
    def _():
        buf_ref[0:HALO, :] = jnp.zeros((HALO, buf_ref.shape[1]), F32)

    buf_ref[HALO:HALO + tq, :] = c_ref[...].astype(F32) * x_ref[...].astype(F32)
    v = jnp.zeros((tq, buf_ref.shape[1]), F32)
    for j in range(SC_CONV):
        r0 = HALO - (SC_CONV - 1) + j
        v = v + w_ref[j:j + 1, :] * buf_ref[r0:r0 + tq, :]
    o_ref[...] = (b_ref[...].astype(F32) * v).astype(BF16)
    buf_ref[0:HALO, :] = buf_ref[tq:tq + HALO, :]


def _sconv(proj, w, b, s):
    tq = min(512, s)
    nq = s // tq
    d = D_MODEL
    row = lambda bi, i: bi * nq + i
    return pl.pallas_call(
        _sconv_kernel,
        grid=(b, nq),
        in_specs=[pl.BlockSpec((tq, d), lambda bi, i: (row(bi, i), COL_SCB // d)),
                  pl.BlockSpec((tq, d), lambda bi, i: (row(bi, i), COL_SCC // d)),
                  pl.BlockSpec((tq, d), lambda bi, i: (row(bi, i), COL_SCX // d)),
                  pl.BlockSpec((SC_CONV, d), lambda bi, i: (0, 0))],
        out_specs=pl.BlockSpec((tq, d), lambda bi, i: (row(bi, i), 0)),
        out_shape=jax.ShapeDtypeStruct((b * s, d), BF16),
        scratch_shapes=[pltpu.VMEM((tq + 8, d), F32)],
        compiler_params=_cparams(("parallel", "arbitrary")),
        name="sconv",
    )(proj, proj, proj, w)


def _dsa_prep_kernel(ckv_ref, cq_ref, small_ref, qg_ref, kvg_ref, kg_ref, wqi_ref, wuq_ref, wuk_ref,
                     qi_ref, ql_ref, ckvn_ref, kidx_ref):
    def norm(v, g):
        return v * lax.rsqrt(jnp.mean(v * v, axis=-1, keepdims=True) + EPS) * g

    cq = norm(cq_ref[...].astype(F32), qg_ref[...]).astype(BF16)
    ckvn_ref[...] = norm(ckv_ref[...].astype(F32), kvg_ref[...]).astype(BF16)
    sm = small_ref[...]
    lane = lax.broadcasted_iota(I32, sm.shape, 1)
    ksq = jnp.where(lane >= SM_KIDX, sm * sm, 0.0)
    kidx_ref[...] = (sm * lax.rsqrt(jnp.sum(ksq, axis=-1, keepdims=True) * (1.0 / D_IDX) + EPS) * kg_ref[...]).astype(BF16)
    qi = jnp.dot(cq, wqi_ref[...], preferred_element_type=F32)
    for h in range(N_IDX_HEADS):
        qi_ref[0, h] = qi[:, h * LANES:(h + 1) * LANES].astype(BF16)
    q = jnp.dot(cq, wuq_ref[...], preferred_element_type=F32).astype(BF16)
    for h in range(N_ATT_HEADS):
        ql = jnp.dot(q[:, h * LANES:(h + 1) * LANES], wuk_ref[h], preferred_element_type=F32)
        ql_ref[0, h] = (ql * ATT_SCALE).astype(BF16)


def _dsa_prep(proj, small, p, b, s):
    tm = min(512, s)
    nq = s // tm
    row = lambda bi, i: bi * nq + i
    full = lambda shape: pl.BlockSpec(shape, lambda bi, i: tuple(0 for _ in shape))
    return pl.pallas_call(
        _dsa_prep_kernel,
        grid=(b, nq),
        in_specs=[pl.BlockSpec((tm, KV_LORA), lambda bi, i: (row(bi, i), COL_CKV // KV_LORA)),
                  pl.BlockSpec((tm, Q_LORA), lambda bi, i: (row(bi, i), COL_CQ // Q_LORA)),
                  pl.BlockSpec((tm, LANES), lambda bi, i: (row(bi, i), 0)),
                  full((1, Q_LORA)), full((1, KV_LORA)), full((1, LANES)),
                  full((Q_LORA, N_IDX_HEADS * LANES)), full((Q_LORA, N_ATT_HEADS * LANES)),
                  full((N_ATT_HEADS, LANES, KV_LORA))],
        out_specs=[pl.BlockSpec((1, N_IDX_HEADS, tm, LANES), lambda bi, i: (bi, 0, i, 0)),
                   pl.BlockSpec((1, N_ATT_HEADS, tm, KV_LORA), lambda bi, i: (bi, 0, i, 0)),
                   pl.BlockSpec((tm, KV_LORA), lambda bi, i: (row(bi, i), 0)),
                   pl.BlockSpec((tm, LANES), lambda bi, i: (row(bi, i), 0))],
        out_shape=[jax.ShapeDtypeStruct((b, N_IDX_HEADS, s, LANES), BF16),
                   jax.ShapeDtypeStruct((b, N_ATT_HEADS, s, KV_LORA), BF16),
                   jax.ShapeDtypeStruct((b * s, KV_LORA), BF16),
                   jax.ShapeDtypeStruct((b * s, LANES), BF16)],
        compiler_params=_cparams(("parallel", "parallel")),
        name="dsa_prep",
    )(proj, proj, small, p["q_norm_g"], p["kv_norm_g"], p["k_norm_g"], p["w_qidx"], p["w_uq"], p["w_uk"])


def _dsa_kernel(qi_ref, ql_ref, small_ref, kidx_ref, ckv_ref, wuv_ref, o_ref,
                key_ref, whb_ref, s_ref, p_ref, bias_ref, m_ref, l_ref, alpha_ref, acc_ref, *, topk, seq_bits):
    qb = pl.program_id(1)
    QB = Q_BLOCK
    TK = key_ref.shape[2]
    NT = TK // LANES
    nkc = (qb * QB + QB + TK - 1) // TK
    q_pos = qb * QB + lax.broadcasted_iota(I32, (QB, 1), 0)
    lane_tk = lax.broadcasted_iota(I32, (QB, TK), 1)

    sm = small_ref[...]
    for h in range(N_IDX_HEADS):
        whb_ref[h] = jnp.broadcast_to(sm[:, SM_WIDX + h:SM_WIDX + h + 1] * IDX_SCALE, (QB, LANES))
    qi = qi_ref[0].reshape(N_IDX_HEADS * QB, LANES)

    def score_chunk(kc, carry):
        off = pl.multiple_of(kc * TK, TK)
        kx = kidx_ref[pl.ds(off, TK), :]
        s_ref[0:N_IDX_HEADS * QB, :] = lax.dot_general(qi, kx, _NT, preferred_element_type=F32)
        for t in range(NT):
            sc = jnp.zeros((QB, LANES), F32)
            for h in range(N_IDX_HEADS):
                d = s_ref[h * QB:(h + 1) * QB, t * LANES:(t + 1) * LANES]
                sc = sc + jnp.maximum(d, 0.0) * whb_ref[h]
            bits = pltpu.bitcast(sc, I32)
            key = bits ^ ((bits >> 31) & 0x7FFFFFFF)
            kpos = off + t * LANES + lax.broadcasted_iota(I32, (QB, LANES), 1)
            key_ref[kc, :, t * LANES:(t + 1) * LANES] = jnp.where(kpos <= q_pos, key, KEY_MIN)
        return carry

    lax.fori_loop(0, nkc, score_chunk, 0)

    kk = jnp.minimum(topk, q_pos + 1).astype(F32)

    def count(pred):
        def body(kc, acc):
            k = key_ref[kc]
            for t in range(NT):
                acc = acc + jnp.where(pred(k[:, t * LANES:(t + 1) * LANES], kc * TK + t * LANES), 1.0, 0.0)
            return acc
        acc = lax.fori_loop(0, nkc, body, jnp.zeros((QB, LANES), F32))
        return jnp.sum(acc, axis=-1, keepdims=True)

    def value_bit(i, u):
        cand_u = u | jnp.left_shift(jnp.int32(1), 31 - i)
        cand = cand_u ^ KEY_MIN
        cnt = count(lambda k, base: k >= cand)
        return jnp.where(cnt >= kk, cand_u, u)

    thr = lax.fori_loop(0, 32, value_bit, jnp.zeros((QB, 1), I32)) ^ KEY_MIN
    c_gt = count(lambda k, base: k > thr)
    c_ge = count(lambda k, base: k >= thr)
    need = kk - c_gt
    lane128 = lax.broadcasted_iota(I32, (QB, LANES), 1)

    def tie_cut():
        def index_bit(i, pcut):
            cand = pcut | jnp.left_shift(jnp.int32(1), seq_bits - 1 - i)
            cnt = count(lambda k, base: (k == thr) & (base + lane128 < cand))
            return jnp.where(cnt < need, cand, pcut)
        return lax.fori_loop(0, seq_bits, index_bit, jnp.zeros((QB, 1), I32))

    has_ties = jnp.max(c_ge - kk) > 0.0
    pcut = lax.cond(has_ties, tie_cut, lambda: jnp.full((QB, 1), 2 ** seq_bits - 1, I32))

    m_ref[...] = jnp.full(m_ref.shape, NEG_BIG, F32)
    l_ref[...] = jnp.zeros(l_ref.shape, F32)
    acc_ref[...] = jnp.zeros(acc_ref.shape, F32)
    ql = ql_ref[0].reshape(N_ATT_HEADS * QB, KV_LORA)

    def attend_chunk(kc, carry):
        off = pl.multiple_of(kc * TK, TK)
        kv = ckv_ref[pl.ds(off, TK), :]
        k = key_ref[kc]
        sel = (k > thr) | ((k == thr) & (off + lane_tk <= pcut))
        bias_ref[...] = jnp.where(sel, 0.0, NEG_BIG)
        s_ref[...] = lax.dot_general(ql, kv, _NT, preferred_element_type=F32)
        for h in range(N_ATT_HEADS):
            rows = slice(h * QB, (h + 1) * QB)
            s = s_ref[rows, :] + bias_ref[...]
            m_prev = m_ref[rows, :]
            m_new = jnp.maximum(m_prev, jnp.max(s, axis=-1, keepdims=True))
            alpha = jnp.exp(m_prev - m_new)
            p = jnp.exp(s - m_new)
            l_ref[rows, :] = alpha * l_ref[rows, :] + jnp.sum(p, axis=-1, keepdims=True)
            m_ref[rows, :] = m_new
            alpha_ref[rows, :] = alpha
            p_ref[rows, :] = p.astype(BF16)
        pv = jnp.dot(p_ref[...], kv, preferred_element_type=F32)
        acc_ref[...] = acc_ref[...] * alpha_ref[...] + pv
        return carry

    lax.fori_loop(0, nkc, attend_chunk, 0)

    for pr in range(N_ATT_HEADS // 2):
        out = jnp.zeros((QB, LANES), F32)
        for half in range(2):
            h = 2 * pr + half
            rows = slice(h * QB, (h + 1) * QB)
            o_lat = (acc_ref[rows, :] / l_ref[rows, :]).astype(BF16)
            out = out + jnp.dot(o_lat, wuv_ref[h], preferred_element_type=F32)
        o_ref[:, pr * LANES:(pr + 1) * LANES] = out.astype(BF16)


def _dsa(qi, ql, small, kidx, ckvn, wuv, b, s):
    QB = Q_BLOCK
    TK = min(DSA_TK, s)
    nq = s // QB
    topk = min(TOPK_MAX, s // 4)
    seq_bits = (s - 1).bit_length()
    resident = dict(pipeline_mode=pl.Buffered(1))
    return pl.pallas_call(
        functools.partial(_dsa_kernel, topk=topk, seq_bits=seq_bits),
        grid=(b, nq),
        in_specs=[pl.BlockSpec((1, N_IDX_HEADS, QB, LANES), lambda bi, i: (bi, 0, i, 0)),
                  pl.BlockSpec((1, N_ATT_HEADS, QB, KV_LORA), lambda bi, i: (bi, 0, i, 0)),
                  pl.BlockSpec((QB, LANES), lambda bi, i: (bi * nq + i, 0)),
                  pl.BlockSpec((s, LANES), lambda bi, i: (bi, 0), **resident),
                  pl.BlockSpec((s, KV_LORA), lambda bi, i: (bi, 0), **resident),
                  pl.BlockSpec((N_ATT_HEADS, KV_LORA, LANES), lambda bi, i: (0, 0, 0))],
        out_specs=pl.BlockSpec((QB, N_ATT_HEADS * ATT_V_DIM), lambda bi, i: (bi * nq + i, 0)),
        out_shape=jax.ShapeDtypeStruct((b * s, N_ATT_HEADS * ATT_V_DIM), BF16),
        scratch_shapes=[pltpu.VMEM((s // TK, QB, TK), I32),
                        pltpu.VMEM((N_IDX_HEADS, QB, LANES), F32),
                        pltpu.VMEM((N_ATT_HEADS * QB, TK), F32),
                        pltpu.VMEM((N_ATT_HEADS * QB, TK), BF16),
                        pltpu.VMEM((QB, TK), F32),
                        pltpu.VMEM((N_ATT_HEADS * QB, 1), F32),
                        pltpu.VMEM((N_ATT_HEADS * QB, 1), F32),
                        pltpu.VMEM((N_ATT_HEADS * QB, 1), F32),
                        pltpu.VMEM((N_ATT_HEADS * QB, KV_LORA), F32)],
        compiler_params=_cparams(("parallel", "arbitrary")),
        name="dsa",
    )(qi, ql, small, kidx, ckvn, wuv)


def _merge_kernel(x_ref, ya_ref, yb_ref, yc_ref, ga_ref, gb_ref, gc_ref, wa_ref, wb_ref, wc_ref, wm_ref, o_ref):
    def branch(y_ref, w_ref, g_ref):
        y = jnp.dot(y_ref[...], w_ref[...], preferred_element_type=F32)
        return jax.nn.sigmoid(g_ref[...].astype(F32)) * y

    merged = branch(ya_ref, wa_ref, ga_ref) + branch(yb_ref, wb_ref, gb_ref) + branch(yc_ref, wc_ref, gc_ref)
    o_ref[...] = x_ref[...] + jnp.dot(merged.astype(BF16), wm_ref[...], preferred_element_type=F32)


def _merge(x2, ya, yb, yc, proj, p):
    t, d = x2.shape
    tm = min(512, t)
    tok = lambda cols, cb: pl.BlockSpec((tm, cols), lambda i: (i, cb))
    full = lambda shape: pl.BlockSpec(shape, lambda i: (0, 0))
    return pl.pallas_call(
        _merge_kernel,
        grid=(t // tm,),
        in_specs=[tok(d, 0), tok(D_INNER, 0), tok(d, 0), tok(d, 0),
                  tok(d, COL_GA // d), tok(d, COL_GB // d), tok(d, COL_GC // d),
                  full((D_INNER, d)), full((d, d)), full((d, d)), full((d, d))],
        out_specs=tok(d, 0),
        out_shape=jax.ShapeDtypeStruct((t, d), F32),
        compiler_params=_cparams(("parallel",)),
        name="merge",
    )(x2, ya, yb, yc, proj, proj, proj, p["ssm_w_out"], p["sc_w_out"], p["att_w_out"], p["mix_w_out"])


def _first_argmax(v, lane):
    m = jnp.max(v, axis=-1, keepdims=True)
    idx = jnp.min(jnp.where(v == m, lane, LANES), axis=-1, keepdims=True)
    return m, idx


def _moe_kernel(x_ref, g_ref, wr_ref, br_ref, wg_ref, wu_ref, wd_ref, fg_ref, o_ref, h_ref, comb_ref, acc_ref,
                *, final_norm):
    e = pl.program_id(1)

    @pl.when(e == 0)
    def _():
        x = x_ref[...]
        h = x * lax.rsqrt(jnp.mean(x * x, axis=-1, keepdims=True) + EPS) * g_ref[...]
        h_ref[...] = h.astype(BF16)
        logits = jnp.dot(h, wr_ref[...], precision=lax.Precision.HIGHEST, preferred_element_type=F32) + br_ref[...]
        lane = lax.broadcasted_iota(I32, logits.shape, 1)
        neg_inf = -jnp.inf
        lg = jnp.where(lane < N_EXPERT_GROUPS, logits, neg_inf)
        eg = jnp.exp(lg - jnp.max(lg, axis=-1, keepdims=True))
        pg = eg / jnp.sum(eg, axis=-1, keepdims=True)
        p_group, g_sel = _first_argmax(jnp.where(lane < N_EXPERT_GROUPS, pg, neg_inf), lane)
        lo = N_EXPERT_GROUPS + g_sel * EXPERTS_PER_GROUP
        in_group = (lane >= lo) & (lane < lo + EXPERTS_PER_GROUP)
        le = jnp.where(in_group, logits, neg_inf)
        ee = jnp.exp(le - jnp.max(le, axis=-1, keepdims=True))
        pe = jnp.where(in_group, ee / jnp.sum(ee, axis=-1, keepdims=True), neg_inf)
        p1, i1 = _first_argmax(pe, lane)
        p2, i2 = _first_argmax(jnp.where(lane == i1, neg_inf, pe), lane)
        tot = p1 + p2
        comb_ref[...] = (jnp.where(lane == i1, p_group * (p1 / tot), 0.0)
                         + jnp.where(lane == i2, p_group * (p2 / tot), 0.0))
        acc_ref[...] = jnp.zeros(acc_ref.shape, F32)

    h = h_ref[...]
    lane = lax.broadcasted_iota(I32, comb_ref.shape, 1)
    c_e = jnp.sum(jnp.where(lane == e + N_EXPERT_GROUPS, comb_ref[...], 0.0), axis=-1, keepdims=True)
    gate = jnp.dot(h, wg_ref[0], preferred_element_type=F32)
    up = jnp.dot(h, wu_ref[0], preferred_element_type=F32)
    hid = (_silu(gate) * up * c_e).astype(BF16)
    acc_ref[...] += jnp.dot(hid, wd_ref[0], preferred_element_type=F32)

    @pl.when(e == N_EXPERTS - 1)
    def _():
        y = x_ref[...] + acc_ref[...]
        if final_norm:
            y = y * lax.rsqrt(jnp.mean(y * y, axis=-1, keepdims=True) + EPS) * fg_ref[...]
        o_ref[...] = y


def _moe(x2, p, final_g, final_norm):
    t, d = x2.shape
    tm = min(1024, t)
    full = lambda shape: pl.BlockSpec(shape, lambda i, e: (0, 0))
    return pl.pallas_call(
        functools.partial(_moe_kernel, final_norm=final_norm),
        grid=(t // tm, N_EXPERTS),
        in_specs=[pl.BlockSpec((tm, d), lambda i, e: (i, 0)), full((1, d)), full((d, LANES)), full((1, LANES)),
                  pl.BlockSpec((1, d, D_EXPERT), lambda i, e: (e, 0, 0)),
                  pl.BlockSpec((1, d, D_EXPERT), lambda i, e: (e, 0, 0)),
                  pl.BlockSpec((1, D_EXPERT, d), lambda i, e: (e, 0, 0)),
                  full((1, d))],
        out_specs=pl.BlockSpec((tm, d), lambda i, e: (i, 0)),
        out_shape=jax.ShapeDtypeStruct((t, d), F32),
        scratch_shapes=[pltpu.VMEM((tm, d), BF16), pltpu.VMEM((tm, LANES), F32), pltpu.VMEM((tm, d), F32)],
        compiler_params=_cparams(("parallel", "arbitrary")),
        name="moe",
    )(x2, p["norm2_g"], p["w_router"], p["b_router"], p["w_gate"], p["w_up"], p["w_down"], final_g)


def _pad_cols(w, width, offset, total):
    return jnp.pad(w, [(0, 0)] * (w.ndim - 1) + [(offset, total - offset - width)])


def _layer_params(l, norm1_g, w_in, ssm_conv_w, ssm_conv_b, ssm_dt_bias, ssm_a_log, ssm_d, ssm_norm_g, ssm_w_out,
                  sc_conv_w, sc_w_out, att_q_norm_g, att_kv_norm_g, idx_k_norm_g, att_w_uq, idx_w_q, att_w_uk,
                  att_w_uv, att_w_out, mix_w_out, norm2_g, moe_w_group, moe_b_group, moe_w_expert, moe_b_expert,
                  moe_w_gate, moe_w_up, moe_w_down):
    w = w_in[l]
    o = 0
    seg = {}
    for name, size in (("z", 2048), ("xbc", 3072), ("dt", 32), ("scb", 1024), ("scc", 1024), ("scx", 1024),
                       ("cq", Q_LORA), ("ckv", KV_LORA), ("kidx", D_IDX), ("widx", N_IDX_HEADS),
                       ("ga", 1024), ("gb", 1024), ("gc", 1024)):
        seg[name] = w[:, o:o + size]
        o += size
    d = w.shape[0]
    small = jnp.concatenate([seg["dt"], seg["widx"], jnp.zeros((d, SM_KIDX - SM_WIDX - N_IDX_HEADS), F32),
                             seg["kidx"]], axis=1)
    w_perm = jnp.concatenate([seg["z"], seg["xbc"], seg["scb"], seg["scc"], seg["scx"], seg["ga"], seg["gb"],
                              seg["gc"], seg["ckv"], seg["cq"], small,
                              jnp.zeros((d, N_PROJ - COL_SMALL - LANES), F32)], axis=1).astype(BF16)
    head_of_lane = jnp.arange(LANES)[:, None]
    head_of_chan = (jnp.arange(D_INNER) // SSM_HEAD_DIM)[None, :]
    wqi = idx_w_q[l]
    wuq = att_w_uq[l]
    wuv = att_w_uv[l]
    wuv_pad = jnp.stack([_pad_cols(wuv[h], ATT_V_DIM, (h % 2) * ATT_V_DIM, LANES) for h in range(N_ATT_HEADS)])
    return {
        "norm1_g": norm1_g[l][None, :], "w_perm": w_perm,
        "conv_w": ssm_conv_w[l], "conv_b": ssm_conv_b[l][None, :],
        "dt_b": _pad_cols(ssm_dt_bias[l][None, :], N_SSM_HEADS, SM_DT, LANES), "dt_bt": ssm_dt_bias[l][:, None],
        "a_log": _pad_cols(ssm_a_log[l][None, :], N_SSM_HEADS, SM_DT, LANES), "a_logt": ssm_a_log[l][:, None],
        "d_skip": jnp.repeat(ssm_d[l], SSM_HEAD_DIM)[None, :], "norm_g": ssm_norm_g[l][None, :],
        "expand": (head_of_lane == head_of_chan).astype(BF16),
        "sc_conv_w": sc_conv_w[l],
        "q_norm_g": att_q_norm_g[l][None, :], "kv_norm_g": att_kv_norm_g[l][None, :],
        "k_norm_g": _pad_cols(idx_k_norm_g[l][None, :], D_IDX, SM_KIDX, LANES),
        "w_qidx": _pad_cols(wqi, D_IDX, SM_KIDX, LANES).reshape(Q_LORA, N_IDX_HEADS * LANES).astype(BF16),
        "w_uq": _pad_cols(wuq, ATT_HEAD_DIM, 0, LANES).reshape(Q_LORA, N_ATT_HEADS * LANES).astype(BF16),
        "w_uk": jnp.pad(att_w_uk[l], ((0, 0), (0, LANES - ATT_HEAD_DIM), (0, 0))).astype(BF16),
        "w_uv": wuv_pad.astype(BF16),
        "ssm_w_out": ssm_w_out[l].astype(BF16), "sc_w_out": sc_w_out[l].astype(BF16),
        "att_w_out": att_w_out[l].astype(BF16), "mix_w_out": mix_w_out[l].astype(BF16),
        "norm2_g": norm2_g[l][None, :],
        "w_router": _pad_cols(jnp.concatenate([moe_w_group[l], moe_w_expert[l]], axis=1),
                              N_EXPERT_GROUPS + N_EXPERTS, 0, LANES),
        "b_router": _pad_cols(jnp.concatenate([moe_b_group[l], moe_b_expert[l]])[None, :],
                              N_EXPERT_GROUPS + N_EXPERTS, 0, LANES),
        "w_gate": moe_w_gate[l].astype(BF16), "w_up": moe_w_up[l].astype(BF16), "w_down": moe_w_down[l].astype(BF16),
    }


def kernel(x, norm1_g, w_in, ssm_conv_w, ssm_conv_b, ssm_dt_bias, ssm_a_log, ssm_d, ssm_norm_g, ssm_w_out, sc_conv_w, sc_w_out, att_q_norm_g, att_kv_norm_g, idx_k_norm_g, att_w_uq, idx_w_q, att_w_uk, att_w_uv, att_w_out, mix_w_out, norm2_g, moe_w_group, moe_b_group, moe_w_expert, moe_b_expert, moe_w_gate, moe_w_up, moe_w_down, final_norm_g):
    b, s, d = x.shape
    depth = w_in.shape[0]
    assert d == D_MODEL and s % DSA_TK == 0 or s < DSA_TK
    x2 = x.reshape(b * s, d)
    final_g = final_norm_g[None, :]
    for l in range(depth):
        p = _layer_params(l, norm1_g, w_in, ssm_conv_w, ssm_conv_b, ssm_dt_bias, ssm_a_log, ssm_d, ssm_norm_g,
                          ssm_w_out, sc_conv_w, sc_w_out, att_q_norm_g, att_kv_norm_g, idx_k_norm_g, att_w_uq,
                          idx_w_q, att_w_uk, att_w_uv, att_w_out, mix_w_out, norm2_g, moe_w_group, moe_b_group,
                          moe_w_expert, moe_b_expert, moe_w_gate, moe_w_up, moe_w_down)
        proj, small = _in_proj(x2, p["norm1_g"], p["w_perm"])
        dtt = jnp.swapaxes(small[:, SM_DT:SM_DT + N_SSM_HEADS].reshape(b, s, N_SSM_HEADS), 1, 2)
        ya = _ssd(proj, small, dtt, p, b, s)
        yb = _sconv(proj, p["sc_conv_w"], b, s)
        qi, ql, ckvn, kidx = _dsa_prep(proj, small, p, b, s)
        yc = _dsa(qi, ql, small, kidx, ckvn, p["w_uv"], b, s)
        x2 = _merge(x2, ya, yb, yc, proj, p)
        x2 = _moe(x2, p, final_g, final_norm=(l == depth - 1))
    return x2.reshape(b, s, d)
```

```python
import functools

import jax
import jax.numpy as jnp
from jax import lax
from jax.experimental import pallas as pl
from jax.experimental.pallas import tpu as pltpu

F32 = jnp.float32
BF16 = jnp.bfloat16
I32 = jnp.int32

EPS = 1e-6
LANES = 128
N_SSM_HEADS = 32
SSM_HEAD_DIM = 64
N_SSM_GROUPS = 4
D_STATE = 128
SSM_CONV = 4
SSM_CHUNK = 128
SC_CONV = 3
N_ATT_HEADS = 16
ATT_HEAD_DIM = 64
ATT_V_DIM = 64
Q_LORA = 384
KV_LORA = 256
N_IDX_HEADS = 8
D_IDX = 64
TOPK_MAX = 256
Q_BLOCK = 128
ATT_SCALE = ATT_HEAD_DIM ** -0.5
LOG2_E = 1.4426950408889634
IDX_SCALE = (N_IDX_HEADS * D_IDX) ** -0.5
N_EXPERT_GROUPS = 4
EXPERTS_PER_GROUP = 4
N_EXPERTS = 16
D_EXPERT = 512

D_MODEL = 1024
D_INNER = 2048
COL_Z = 0
COL_XS = 2048
COL_BM = 4096
COL_CM = 4608
COL_SCB = 5120
COL_SCC = 6144
COL_SCX = 7168
COL_GA = 8192
COL_GB = 9216
COL_GC = 10240
COL_CKV = 11264
COL_CQ = 11520
COL_SMALL = 11904
N_PROJ = 12288
SM_DT = 0
SM_WIDX = 32
SM_KIDX = 64

PROJ_TM = 1024
PROJ_TN = 512
DSA_TK = 512
ATT_QUARTER_HEADS = 4
ATT_ROW_BLOCK = 16
KEY_MIN = -2 ** 31
NEG_BIG = -1e30
VMEM_LIMIT = 60 * 1024 * 1024

_NT = (((1,), (1,)), ((), ()))


def _cparams(sem):
    return pltpu.CompilerParams(dimension_semantics=sem, vmem_limit_bytes=VMEM_LIMIT)


def _silu(v):
    return v * jax.nn.sigmoid(v)


def _softplus(v):
    return jnp.maximum(v, 0.0) + jnp.log(1.0 + jnp.exp(-jnp.abs(v)))


def _in_proj_kernel(x_ref, g_ref, w_ref, o_ref, small_ref, hn_ref, *, small_j, small_off):
    j = pl.program_id(1)

    @pl.when(j == 0)
    def _():
        x = x_ref[...]
        hn = x * lax.rsqrt(jnp.mean(x * x, axis=-1, keepdims=True) + EPS) * g_ref[...]
        hn_ref[...] = hn.astype(BF16)

    acc = jnp.dot(hn_ref[...], w_ref[...], preferred_element_type=F32)
    o_ref[...] = acc.astype(BF16)

    @pl.when(j == small_j)
    def _():
        small_ref[...] = acc[:, small_off:small_off + LANES]


def _in_proj(x2, g, w_perm):
    t, d = x2.shape
    tm = min(PROJ_TM, t)
    small_j, small_off = COL_SMALL // PROJ_TN, COL_SMALL % PROJ_TN
    return pl.pallas_call(
        functools.partial(_in_proj_kernel, small_j=small_j, small_off=small_off),
        grid=(t // tm, N_PROJ // PROJ_TN),
        in_specs=[pl.BlockSpec((tm, d), lambda i, j: (i, 0)),
                  pl.BlockSpec((1, d), lambda i, j: (0, 0)),
                  pl.BlockSpec((d, PROJ_TN), lambda i, j: (0, j))],
        out_specs=[pl.BlockSpec((tm, PROJ_TN), lambda i, j: (i, j)),
                   pl.BlockSpec((tm, LANES), lambda i, j: (i, 0))],
        out_shape=[jax.ShapeDtypeStruct((t, N_PROJ), BF16), jax.ShapeDtypeStruct((t, LANES), F32)],
        scratch_shapes=[pltpu.VMEM((tm, d), BF16)],
        compiler_params=_cparams(("parallel", "arbitrary")),
        name="in_proj",
    )(x2, g, w_perm)


def _split_dot(a, e_bf16):
    hi = a.astype(BF16)
    lo = (a - hi.astype(F32)).astype(BF16)
    return (jnp.dot(hi, e_bf16, preferred_element_type=F32) + jnp.dot(lo, e_bf16, preferred_element_type=F32))


def _ssd_kernel(z_ref, xs_ref, bm_ref, cm_ref, small_ref, dtt_ref, convw_ref, convb_ref, dtb_ref, dtbt_ref,
                alog_ref, alogt_ref, dskip_ref, ng_ref, e_ref, o_ref, buf_ref, xbc_ref, state_ref, y_ref):
    c = pl.program_id(1)
    L = SSM_CHUNK
    GW = D_INNER // N_SSM_GROUPS
    HALO = 8

    @pl.when(c == 0)
    def _():
        buf_ref[0:HALO, :] = jnp.zeros((HALO, buf_ref.shape[1]), F32)
        state_ref[...] = jnp.zeros(state_ref.shape, F32)

    buf_ref[HALO:HALO + L, 0:D_INNER] = xs_ref[...].astype(F32)
    buf_ref[HALO:HALO + L, D_INNER:D_INNER + GW] = bm_ref[...].astype(F32)
    buf_ref[HALO:HALO + L, D_INNER + GW:D_INNER + 2 * GW] = cm_ref[...].astype(F32)
    for cc in range(buf_ref.shape[1] // GW):
        cols = slice(cc * GW, (cc + 1) * GW)
        acc = jnp.broadcast_to(convb_ref[:, cols], (L, GW))
        for j in range(SSM_CONV):
            r0 = HALO - (SSM_CONV - 1) + j
            acc = acc + convw_ref[j:j + 1, cols] * buf_ref[r0:r0 + L, cols]
        xbc_ref[:, cols] = _silu(acc)
    buf_ref[0:HALO, :] = buf_ref[L:L + HALO, :]

    row = lax.broadcasted_iota(I32, (L, L), 0)
    col = lax.broadcasted_iota(I32, (L, L), 1)
    causal = row >= col
    tril = causal.astype(F32)
    triu = (row <= col).astype(F32)
    dt = _softplus(small_ref[...] + dtb_ref[...])
    da = dt * (-jnp.exp(alog_ref[...]))
    cs = jnp.dot(tril, da, precision=lax.Precision.HIGHEST, preferred_element_type=F32)
    dtt = _softplus(dtt_ref[0] + dtbt_ref[...])
    dat = dtt * (-jnp.exp(alogt_ref[...]))
    cst = jnp.dot(dat, triu, precision=lax.Precision.HIGHEST, preferred_element_type=F32)
    cs_last = cs[L - 1:L, :]
    e = e_ref[...]
    dt_e = _split_dot(dt, e)
    dte_e = _split_dot(jnp.exp(cs_last - cs), e)
    eo_e = _split_dot(jnp.exp(cs), e)
    cd_e = _split_dot(jnp.broadcast_to(jnp.exp(cs_last), (8, LANES)), e)[0:1, :]

    lane = lax.broadcasted_iota(I32, (L, LANES), 1)
    lo_half = lane < SSM_HEAD_DIM
    for g in range(N_SSM_GROUPS):
        gc = slice(g * GW, (g + 1) * GW)
        bm_g = xbc_ref[:, D_INNER + g * D_STATE:D_INNER + (g + 1) * D_STATE]
        cm_g = xbc_ref[:, D_INNER + GW + g * D_STATE:D_INNER + GW + (g + 1) * D_STATE]
        cm_b = cm_g.astype(BF16)
        cb = lax.dot_general(cm_b, bm_g.astype(BF16), _NT, preferred_element_type=F32)
        xs_g = xbc_ref[:, gc]
        xdt_g = xs_g * dt_e[:, gc]
        st = state_ref[g]
        y_off = jnp.dot(cm_b, st.astype(BF16), preferred_element_type=F32) * eo_e[:, gc]
        for pr in range(GW // LANES):
            x_pair = xdt_g[:, pr * LANES:(pr + 1) * LANES]
            y_pair = y_off[:, pr * LANES:(pr + 1) * LANES]
            for half in range(2):
                h = g * (GW // SSM_HEAD_DIM) + pr * 2 + half
                seg = cs[:, h:h + 1] - cst[h:h + 1, :]
                dec = jnp.exp(jnp.where(causal, seg, -jnp.inf))
                m = (cb * dec).astype(BF16)
                keep = lo_half if half == 0 else jnp.logical_not(lo_half)
                x_h = jnp.where(keep, x_pair, 0.0).astype(BF16)
                y_pair = y_pair + jnp.dot(m, x_h, preferred_element_type=F32)
            y_ref[:, g * GW + pr * LANES:g * GW + (pr + 1) * LANES] = y_pair
        xd = (xdt_g * dte_e[:, gc]).astype(BF16)
        new = jnp.dot(bm_g.T.astype(BF16), xd, preferred_element_type=F32)
        state_ref[g] = cd_e[:, gc] * st + new
        yg = (y_ref[:, gc] + dskip_ref[:, gc] * xs_g) * _silu(z_ref[:, gc].astype(F32))
        yg = yg * lax.rsqrt(jnp.mean(yg * yg, axis=-1, keepdims=True) + EPS)
        o_ref[:, gc] = (yg * ng_ref[:, gc]).astype(BF16)


def _ssd(proj, small, dtt, p, b, s):
    L = SSM_CHUNK
    nc = s // L
    conv_dim = D_INNER + 2 * N_SSM_GROUPS * D_STATE
    row = lambda bi, ci: bi * nc + ci
    full = lambda shape: pl.BlockSpec(shape, lambda bi, ci: tuple(0 for _ in shape))
    return pl.pallas_call(
        _ssd_kernel,
        grid=(b, nc),
        in_specs=[pl.BlockSpec((L, D_INNER), lambda bi, ci: (row(bi, ci), COL_Z // D_INNER)),
                  pl.BlockSpec((L, D_INNER), lambda bi, ci: (row(bi, ci), COL_XS // D_INNER)),
                  pl.BlockSpec((L, 512), lambda bi, ci: (row(bi, ci), COL_BM // 512)),
                  pl.BlockSpec((L, 512), lambda bi, ci: (row(bi, ci), COL_CM // 512)),
                  pl.BlockSpec((L, LANES), lambda bi, ci: (row(bi, ci), 0)),
                  pl.BlockSpec((1, N_SSM_HEADS, L), lambda bi, ci: (bi, 0, ci)),
                  full((SSM_CONV, conv_dim)), full((1, conv_dim)),
                  full((1, LANES)), full((N_SSM_HEADS, 1)), full((1, LANES)), full((N_SSM_HEADS, 1)),
                  full((1, D_INNER)), full((1, D_INNER)), full((LANES, D_INNER))],
        out_specs=pl.BlockSpec((L, D_INNER), lambda bi, ci: (row(bi, ci), 0)),
        out_shape=jax.ShapeDtypeStruct((b * s, D_INNER), BF16),
        scratch_shapes=[pltpu.VMEM((L + 8, conv_dim), F32), pltpu.VMEM((L, conv_dim), F32),
                        pltpu.VMEM((N_SSM_GROUPS, D_STATE, D_INNER // N_SSM_GROUPS), F32),
                        pltpu.VMEM((L, D_INNER), F32)],
        compiler_params=_cparams(("parallel", "arbitrary")),
        name="ssd",
    )(proj, proj, proj, proj, small, dtt, p["conv_w"], p["conv_b"], p["dt_b"], p["dt_bt"], p["a_log"], p["a_logt"],
      p["d_skip"], p["norm_g"], p["expand"])


def _sconv_kernel(b_ref, c_ref, x_ref, w_ref, o_ref, buf_ref):
    i = pl.program_id(1)
    tq = b_ref.shape[0]
    HALO = 8

    @pl.when(i == 0)
    def _():
        buf_ref[0:HALO, :] = jnp.zeros((HALO, buf_ref.shape[1]), F32)

    buf_ref[HALO:HALO + tq, :] = c_ref[...].astype(F32) * x_ref[...].astype(F32)
    v = jnp.zeros((tq, buf_ref.shape[1]), F32)
    for j in range(SC_CONV):
        r0 = HALO - (SC_CONV - 1) + j
        v = v + w_ref[j:j + 1, :] * buf_ref[r0:r0 + tq, :]
    o_ref[...] = (b_ref[...].astype(F32) * v).astype(BF16)
    buf_ref[0:HALO, :] = buf_ref[tq:tq + HALO, :]


def _sconv(proj, w, b, s):
    tq = min(512, s)
    nq = s // tq
    d = D_MODEL
    row = lambda bi, i: bi * nq + i
    return pl.pallas_call(
        _sconv_kernel,
        grid=(b, nq),
        in_specs=[pl.BlockSpec((tq, d), lambda bi, i: (row(bi, i), COL_SCB // d)),
                  pl.BlockSpec((tq, d), lambda bi, i: (row(bi, i), COL_SCC // d)),
                  pl.BlockSpec((tq, d), lambda bi, i: (row(bi, i), COL_SCX // d)),
                  pl.BlockSpec((SC_CONV, d), lambda bi, i: (0, 0))],
        out_specs=pl.BlockSpec((tq, d), lambda bi, i: (row(bi, i), 0)),
        out_shape=jax.ShapeDtypeStruct((b * s, d), BF16),
        scratch_shapes=[pltpu.VMEM((tq + 8, d), F32)],
        compiler_params=_cparams(("parallel", "arbitrary")),
        name="sconv",
    )(proj, proj, proj, w)


def _dsa_prep_kernel(ckv_ref, cq_ref, small_ref, qg_ref, kvg_ref, kg_ref, wqi_ref, wuq_ref, wuk_ref,
                     qi_ref, ql_ref, ckvn_ref, kidx_ref):
    def norm(v, g):
        return v * lax.rsqrt(jnp.mean(v * v, axis=-1, keepdims=True) + EPS) * g

    cq = norm(cq_ref[...].astype(F32), qg_ref[...]).astype(BF16)
    ckvn_ref[...] = norm(ckv_ref[...].astype(F32), kvg_ref[...]).astype(BF16)
    sm = small_ref[...]
    lane = lax.broadcasted_iota(I32, sm.shape, 1)
    ksq = jnp.where(lane >= SM_KIDX, sm * sm, 0.0)
    kidx_ref[...] = (sm * lax.rsqrt(jnp.sum(ksq, axis=-1, keepdims=True) * (1.0 / D_IDX) + EPS) * kg_ref[...]).astype(BF16)
    qi = jnp.dot(cq, wqi_ref[...], preferred_element_type=F32)
    for h in range(N_IDX_HEADS):
        qi_ref[0, h] = qi[:, h * LANES:(h + 1) * LANES].astype(BF16)
    q = jnp.dot(cq, wuq_ref[...], preferred_element_type=F32).astype(BF16)
    for h in range(N_ATT_HEADS):
        ql = jnp.dot(q[:, h * LANES:(h + 1) * LANES], wuk_ref[h], preferred_element_type=F32)
        ql_ref[0, h] = (ql * (ATT_SCALE * LOG2_E)).astype(BF16)


def _dsa_prep(proj, small, p, b, s):
    tm = min(512, s)
    nq = s // tm
    row = lambda bi, i: bi * nq + i
    full = lambda shape: pl.BlockSpec(shape, lambda bi, i: tuple(0 for _ in shape))
    return pl.pallas_call(
        _dsa_prep_kernel,
        grid=(b, nq),
        in_specs=[pl.BlockSpec((tm, KV_LORA), lambda bi, i: (row(bi, i), COL_CKV // KV_LORA)),
                  pl.BlockSpec((tm, Q_LORA), lambda bi, i: (row(bi, i), COL_CQ // Q_LORA)),
                  pl.BlockSpec((tm, LANES), lambda bi, i: (row(bi, i), 0)),
                  full((1, Q_LORA)), full((1, KV_LORA)), full((1, LANES)),
                  full((Q_LORA, N_IDX_HEADS * LANES)), full((Q_LORA, N_ATT_HEADS * LANES)),
                  full((N_ATT_HEADS, LANES, KV_LORA))],
        out_specs=[pl.BlockSpec((1, N_IDX_HEADS, tm, LANES), lambda bi, i: (bi, 0, i, 0)),
                   pl.BlockSpec((1, N_ATT_HEADS, tm, KV_LORA), lambda bi, i: (bi, 0, i, 0)),
                   pl.BlockSpec((tm, KV_LORA), lambda bi, i: (row(bi, i), 0)),
                   pl.BlockSpec((tm, LANES), lambda bi, i: (row(bi, i), 0))],
        out_shape=[jax.ShapeDtypeStruct((b, N_IDX_HEADS, s, LANES), BF16),
                   jax.ShapeDtypeStruct((b, N_ATT_HEADS, s, KV_LORA), BF16),
                   jax.ShapeDtypeStruct((b * s, KV_LORA), BF16),
                   jax.ShapeDtypeStruct((b * s, LANES), BF16)],
        compiler_params=_cparams(("parallel", "parallel")),
        name="dsa_prep",
    )(proj, proj, small, p["q_norm_g"], p["kv_norm_g"], p["k_norm_g"], p["w_qidx"], p["w_uq"], p["w_uk"])


def _dsa_kernel(qi_ref, ql_ref, small_ref, kidx_ref, ckv_ref, wuv_ref, o_ref,
                key_ref, whb_ref, d_ref, s0_ref, s1_ref, s2_ref, s3_ref, p0_ref, p1_ref, p2_ref, p3_ref, bias_ref,
                m_ref, l_ref, rmax_ref, acc_ref, *, topk, seq_bits):
    qb = pl.program_id(1)
    QB = Q_BLOCK
    TK = key_ref.shape[2]
    NT = TK // LANES
    nkc = (qb * QB + QB + TK - 1) // TK
    q_pos = qb * QB + lax.broadcasted_iota(I32, (QB, 1), 0)
    lane_tk = lax.broadcasted_iota(I32, (QB, TK), 1)

    sm = small_ref[...]
    for h in range(N_IDX_HEADS):
        whb_ref[h] = jnp.broadcast_to(sm[:, SM_WIDX + h:SM_WIDX + h + 1] * IDX_SCALE, (QB, LANES))
    qi = qi_ref[0].reshape(N_IDX_HEADS * QB, LANES)

    def score_chunk(kc, carry):
        off = pl.multiple_of(kc * TK, TK)
        kx = kidx_ref[pl.ds(off, TK), :]
        d_ref[...] = lax.dot_general(qi, kx, _NT, preferred_element_type=F32)
        for t in range(NT):
            sc = jnp.zeros((QB, LANES), F32)
            for h in range(N_IDX_HEADS):
                d = d_ref[h * QB:(h + 1) * QB, t * LANES:(t + 1) * LANES]
                sc = sc + jnp.maximum(d, 0.0) * whb_ref[h]
            bits = pltpu.bitcast(sc, I32)
            key = bits ^ ((bits >> 31) & 0x7FFFFFFF)
            kpos = off + t * LANES + lax.broadcasted_iota(I32, (QB, LANES), 1)
            key_ref[kc, :, t * LANES:(t + 1) * LANES] = jnp.where(kpos <= q_pos, key, KEY_MIN)
        return carry

    lax.fori_loop(0, nkc, score_chunk, 0)

    kk = jnp.minimum(topk, q_pos + 1).astype(F32)

    def count(pred):
        def body(kc, acc):
            k = key_ref[kc]
            for t in range(NT):
                acc = acc + jnp.where(pred(k[:, t * LANES:(t + 1) * LANES], kc * TK + t * LANES), 1.0, 0.0)
            return acc
        acc = lax.fori_loop(0, nkc, body, jnp.zeros((QB, LANES), F32))
        return jnp.sum(acc, axis=-1, keepdims=True)

    def value_bit(i, u):
        cand_u = u | jnp.left_shift(jnp.int32(1), 31 - i)
        cand = cand_u ^ KEY_MIN
        cnt = count(lambda k, base: k >= cand)
        return jnp.where(cnt >= kk, cand_u, u)

    thr = lax.fori_loop(0, 32, value_bit, jnp.zeros((QB, 1), I32)) ^ KEY_MIN
    c_gt = count(lambda k, base: k > thr)
    c_ge = count(lambda k, base: k >= thr)
    need = kk - c_gt
    lane128 = lax.broadcasted_iota(I32, (QB, LANES), 1)

    def tie_cut():
        def index_bit(i, pcut):
            cand = pcut | jnp.left_shift(jnp.int32(1), seq_bits - 1 - i)
            cnt = count(lambda k, base: (k == thr) & (base + lane128 < cand))
            return jnp.where(cnt < need, cand, pcut)
        return lax.fori_loop(0, seq_bits, index_bit, jnp.zeros((QB, 1), I32))

    has_ties = jnp.max(c_ge - kk) > 0.0
    pcut = lax.cond(has_ties, tie_cut, lambda: jnp.full((QB, 1), 2 ** seq_bits - 1, I32))

    m_ref[...] = jnp.full(m_ref.shape, NEG_BIG, F32)
    l_ref[...] = jnp.zeros(l_ref.shape, F32)
    acc_ref[...] = jnp.zeros(acc_ref.shape, F32)
    HQ = ATT_QUARTER_HEADS
    n_quarters = N_ATT_HEADS // HQ
    RB = ATT_ROW_BLOCK
    n_rb = QB // RB
    s_refs = (s0_ref, s1_ref, s2_ref, s3_ref)
    p_refs = (p0_ref, p1_ref, p2_ref, p3_ref)

    def attend_chunk(kc, carry):
        off = pl.multiple_of(kc * TK, TK)
        kv = ckv_ref[pl.ds(off, TK), :]
        k = key_ref[kc]
        sel = (k > thr) | ((k == thr) & (off + lane_tk <= pcut))
        bias_ref[...] = jnp.where(sel, 0.0, NEG_BIG)
        def logits(qt):
            ql = ql_ref[0, qt * HQ:(qt + 1) * HQ].reshape(HQ * QB, KV_LORA)
            s_refs[qt][...] = lax.dot_general(ql, kv, _NT, preferred_element_type=F32)

        def row_max(qt, rb):
            bias = bias_ref[rb * RB:(rb + 1) * RB, :]
            for hh in range(HQ):
                r0 = hh * QB + rb * RB
                s = s_refs[qt][r0:r0 + RB, :] + bias
                s_refs[qt][r0:r0 + RB, :] = s
                mx = s[:, 0:LANES]
                for t in range(1, NT):
                    mx = jnp.maximum(mx, s[:, t * LANES:(t + 1) * LANES])
                g0 = qt * HQ * QB + r0
                rmax_ref[g0:g0 + RB, :] = jnp.broadcast_to(jnp.max(mx, axis=-1, keepdims=True), (RB, LANES))

        def row_probs(qt, rb):
            for hh in range(HQ):
                r0 = hh * QB + rb * RB
                g0 = qt * HQ * QB + r0
                m_prev = m_ref[g0:g0 + RB, :]
                m_new = jnp.maximum(m_prev, rmax_ref[g0:g0 + RB, :])
                alpha = jnp.exp2(m_prev - m_new)
                lsum = alpha * l_ref[g0:g0 + RB, :]
                for t in range(NT):
                    cols = slice(t * LANES, (t + 1) * LANES)
                    p = jnp.exp2(s_refs[qt][r0:r0 + RB, cols] - m_new)
                    lsum = lsum + p
                    p_refs[qt][r0:r0 + RB, cols] = p.astype(BF16)
                l_ref[g0:g0 + RB, :] = lsum
                m_ref[g0:g0 + RB, :] = m_new
                for c in range(KV_LORA // LANES):
                    cols = slice(c * LANES, (c + 1) * LANES)
                    acc_ref[g0:g0 + RB, cols] = acc_ref[g0:g0 + RB, cols] * alpha

        logits(0)
        for qt in range(n_quarters):
            if qt + 1 < n_quarters:
                logits(qt + 1)
            for rb in range(n_rb):
                row_max(qt, rb)
            for rb in range(n_rb):
                row_probs(qt, rb)
            rows = slice(qt * HQ * QB, (qt + 1) * HQ * QB)
            acc_ref[rows, :] += jnp.dot(p_refs[qt][...], kv, preferred_element_type=F32)
        return carry

    lax.fori_loop(0, nkc, attend_chunk, 0)

    for pr in range(N_ATT_HEADS // 2):
        out = jnp.zeros((QB, LANES), F32)
        for half in range(2):
            h = 2 * pr + half
            rows = slice(h * QB, (h + 1) * QB)
            l_row = jnp.sum(l_ref[rows, :], axis=-1, keepdims=True)
            o_lat = (acc_ref[rows, :] / l_row).astype(BF16)
            out = out + jnp.dot(o_lat, wuv_ref[h], preferred_element_type=F32)
        o_ref[:, pr * LANES:(pr + 1) * LANES] = out.astype(BF16)


def _dsa(qi, ql, small, kidx, ckvn, wuv, b, s):
    QB = Q_BLOCK
    TK = min(DSA_TK, s)
    nq = s // QB
    topk = min(TOPK_MAX, s // 4)
    seq_bits = (s - 1).bit_length()
    resident = dict(pipeline_mode=pl.Buffered(1))
    return pl.pallas_call(
        functools.partial(_dsa_kernel, topk=topk, seq_bits=seq_bits),
        grid=(b, nq),
        in_specs=[pl.BlockSpec((1, N_IDX_HEADS, QB, LANES), lambda bi, i: (bi, 0, i, 0)),
                  pl.BlockSpec((1, N_ATT_HEADS, QB, KV_LORA), lambda bi, i: (bi, 0, i, 0)),
                  pl.BlockSpec((QB, LANES), lambda bi, i: (bi * nq + i, 0)),
                  pl.BlockSpec((s, LANES), lambda bi, i: (bi, 0), **resident),
                  pl.BlockSpec((s, KV_LORA), lambda bi, i: (bi, 0), **resident),
                  pl.BlockSpec((N_ATT_HEADS, KV_LORA, LANES), lambda bi, i: (0, 0, 0))],
        out_specs=pl.BlockSpec((QB, N_ATT_HEADS * ATT_V_DIM), lambda bi, i: (bi * nq + i, 0)),
        out_shape=jax.ShapeDtypeStruct((b * s, N_ATT_HEADS * ATT_V_DIM), BF16),
        scratch_shapes=[pltpu.VMEM((s // TK, QB, TK), I32),
                        pltpu.VMEM((N_IDX_HEADS, QB, LANES), F32),
                        pltpu.VMEM((N_IDX_HEADS * QB, TK), F32),
                        *[pltpu.VMEM((ATT_QUARTER_HEADS * QB, TK), F32)] * 4,
                        *[pltpu.VMEM((ATT_QUARTER_HEADS * QB, TK), BF16)] * 4,
                        pltpu.VMEM((QB, TK), F32),
                        pltpu.VMEM((N_ATT_HEADS * QB, LANES), F32),
                        pltpu.VMEM((N_ATT_HEADS * QB, LANES), F32),
                        pltpu.VMEM((N_ATT_HEADS * QB, LANES), F32),
                        pltpu.VMEM((N_ATT_HEADS * QB, KV_LORA), F32)],
        compiler_params=_cparams(("parallel", "arbitrary")),
        name="dsa",
    )(qi, ql, small, kidx, ckvn, wuv)


def _merge_kernel(x_ref, ya_ref, yb_ref, yc_ref, ga_ref, gb_ref, gc_ref, wa_ref, wb_ref, wc_ref, wm_ref, o_ref):
    def branch(y_ref, w_ref, g_ref):
        y = jnp.dot(y_ref[...], w_ref[...], preferred_element_type=F32)
        return jax.nn.sigmoid(g_ref[...].astype(F32)) * y

    merged = branch(ya_ref, wa_ref, ga_ref) + branch(yb_ref, wb_ref, gb_ref) + branch(yc_ref, wc_ref, gc_ref)
    o_ref[...] = x_ref[...] + jnp.dot(merged.astype(BF16), wm_ref[...], preferred_element_type=F32)


def _merge(x2, ya, yb, yc, proj, p):
    t, d = x2.shape
    tm = min(512, t)
    tok = lambda cols, cb: pl.BlockSpec((tm, cols), lambda i: (i, cb))
    full = lambda shape: pl.BlockSpec(shape, lambda i: (0, 0))
    return pl.pallas_call(
        _merge_kernel,
        grid=(t // tm,),
        in_specs=[tok(d, 0), tok(D_INNER, 0), tok(d, 0), tok(d, 0),
                  tok(d, COL_GA // d), tok(d, COL_GB // d), tok(d, COL_GC // d),
                  full((D_INNER, d)), full((d, d)), full((d, d)), full((d, d))],
        out_specs=tok(d, 0),
        out_shape=jax.ShapeDtypeStruct((t, d), F32),
        compiler_params=_cparams(("parallel",)),
        name="merge",
    )(x2, ya, yb, yc, proj, proj, proj, p["ssm_w_out"], p["sc_w_out"], p["att_w_out"], p["mix_w_out"])


def _first_argmax(v, lane):
    m = jnp.max(v, axis=-1, keepdims=True)
    idx = jnp.min(jnp.where(v == m, lane, LANES), axis=-1, keepdims=True)
    return m, idx


def _moe_kernel(x_ref, g_ref, wr_ref, br_ref, wg_ref, wu_ref, wd_ref, fg_ref, o_ref, h_ref, comb_ref, acc_ref,
                *, final_norm):
    e = pl.program_id(1)

    @pl.when(e == 0)
    def _():
        x = x_ref[...]
        h = x * lax.rsqrt(jnp.mean(x * x, axis=-1, keepdims=True) + EPS) * g_ref[...]
        h_ref[...] = h.astype(BF16)
        logits = jnp.dot(h, wr_ref[...], precision=lax.Precision.HIGHEST, preferred_element_type=F32) + br_ref[...]
        lane = lax.broadcasted_iota(I32, logits.shape, 1)
        neg_inf = -jnp.inf
        lg = jnp.where(lane < N_EXPERT_GROUPS, logits, neg_inf)
        eg = jnp.exp(lg - jnp.max(lg, axis=-1, keepdims=True))
        pg = eg / jnp.sum(eg, axis=-1, keepdims=True)
        p_group, g_sel = _first_argmax(jnp.where(lane < N_EXPERT_GROUPS, pg, neg_inf), lane)
        lo = N_EXPERT_GROUPS + g_sel * EXPERTS_PER_GROUP
        in_group = (lane >= lo) & (lane < lo + EXPERTS_PER_GROUP)
        le = jnp.where(in_group, logits, neg_inf)
        ee = jnp.exp(le - jnp.max(le, axis=-1, keepdims=True))
        pe = jnp.where(in_group, ee / jnp.sum(ee, axis=-1, keepdims=True), neg_inf)
        p1, i1 = _first_argmax(pe, lane)
        p2, i2 = _first_argmax(jnp.where(lane == i1, neg_inf, pe), lane)
        tot = p1 + p2
        comb_ref[...] = (jnp.where(lane == i1, p_group * (p1 / tot), 0.0)
                         + jnp.where(lane == i2, p_group * (p2 / tot), 0.0))
        acc_ref[...] = jnp.zeros(acc_ref.shape, F32)

    h = h_ref[...]
    lane = lax.broadcasted_iota(I32, comb_ref.shape, 1)
    c_e = jnp.sum(jnp.where(lane == e + N_EXPERT_GROUPS, comb_ref[...], 0.0), axis=-1, keepdims=True)
    gate = jnp.dot(h, wg_ref[0], preferred_element_type=F32)
    up = jnp.dot(h, wu_ref[0], preferred_element_type=F32)
    hid = (_silu(gate) * up * c_e).astype(BF16)
    acc_ref[...] += jnp.dot(hid, wd_ref[0], preferred_element_type=F32)

    @pl.when(e == N_EXPERTS - 1)
    def _():
        y = x_ref[...] + acc_ref[...]
        if final_norm:
            y = y * lax.rsqrt(jnp.mean(y * y, axis=-1, keepdims=True) + EPS) * fg_ref[...]
        o_ref[...] = y


def _moe(x2, p, final_g, final_norm):
    t, d = x2.shape
    tm = min(1024, t)
    full = lambda shape: pl.BlockSpec(shape, lambda i, e: (0, 0))
    return pl.pallas_call(
        functools.partial(_moe_kernel, final_norm=final_norm),
        grid=(t // tm, N_EXPERTS),
        in_specs=[pl.BlockSpec((tm, d), lambda i, e: (i, 0)), full((1, d)), full((d, LANES)), full((1, LANES)),
                  pl.BlockSpec((1, d, D_EXPERT), lambda i, e: (e, 0, 0)),
                  pl.BlockSpec((1, d, D_EXPERT), lambda i, e: (e, 0, 0)),
                  pl.BlockSpec((1, D_EXPERT, d), lambda i, e: (e, 0, 0)),
                  full((1, d))],
        out_specs=pl.BlockSpec((tm, d), lambda i, e: (i, 0)),
        out_shape=jax.ShapeDtypeStruct((t, d), F32),
        scratch_shapes=[pltpu.VMEM((tm, d), BF16), pltpu.VMEM((tm, LANES), F32), pltpu.VMEM((tm, d), F32)],
        compiler_params=_cparams(("parallel", "arbitrary")),
        name="moe",
    )(x2, p["norm2_g"], p["w_router"], p["b_router"], p["w_gate"], p["w_up"], p["w_down"], final_g)


def _pad_cols(w, width, offset, total):
    return jnp.pad(w, [(0, 0)] * (w.ndim - 1) + [(offset, total - offset - width)])


def _layer_params(l, norm1_g, w_in, ssm_conv_w, ssm_conv_b, ssm_dt_bias, ssm_a_log, ssm_d, ssm_norm_g, ssm_w_out,
                  sc_conv_w, sc_w_out, att_q_norm_g, att_kv_norm_g, idx_k_norm_g, att_w_uq, idx_w_q, att_w_uk,
                  att_w_uv, att_w_out, mix_w_out, norm2_g, moe_w_group, moe_b_group, moe_w_expert, moe_b_expert,
                  moe_w_gate, moe_w_up, moe_w_down):
    w = w_in[l]
    o = 0
    seg = {}
    for name, size in (("z", 2048), ("xbc", 3072), ("dt", 32), ("scb", 1024), ("scc", 1024), ("scx", 1024),
                       ("cq", Q_LORA), ("ckv", KV_LORA), ("kidx", D_IDX), ("widx", N_IDX_HEADS),
                       ("ga", 1024), ("gb", 1024), ("gc", 1024)):
        seg[name] = w[:, o:o + size]
        o += size
    d = w.shape[0]
    small = jnp.concatenate([seg["dt"], seg["widx"], jnp.zeros((d, SM_KIDX - SM_WIDX - N_IDX_HEADS), F32),
                             seg["kidx"]], axis=1)
    w_perm = jnp.concatenate([seg["z"], seg["xbc"], seg["scb"], seg["scc"], seg["scx"], seg["ga"], seg["gb"],
                              seg["gc"], seg["ckv"], seg["cq"], small,
                              jnp.zeros((d, N_PROJ - COL_SMALL - LANES), F32)], axis=1).astype(BF16)
    head_of_lane = jnp.arange(LANES)[:, None]
    head_of_chan = (jnp.arange(D_INNER) // SSM_HEAD_DIM)[None, :]
    wqi = idx_w_q[l]
    wuq = att_w_uq[l]
    wuv = att_w_uv[l]
    wuv_pad = jnp.stack([_pad_cols(wuv[h], ATT_V_DIM, (h % 2) * ATT_V_DIM, LANES) for h in range(N_ATT_HEADS)])
    return {
        "norm1_g": norm1_g[l][None, :], "w_perm": w_perm,
        "conv_w": ssm_conv_w[l], "conv_b": ssm_conv_b[l][None, :],
        "dt_b": _pad_cols(ssm_dt_bias[l][None, :], N_SSM_HEADS, SM_DT, LANES), "dt_bt": ssm_dt_bias[l][:, None],
        "a_log": _pad_cols(ssm_a_log[l][None, :], N_SSM_HEADS, SM_DT, LANES), "a_logt": ssm_a_log[l][:, None],
        "d_skip": jnp.repeat(ssm_d[l], SSM_HEAD_DIM)[None, :], "norm_g": ssm_norm_g[l][None, :],
        "expand": (head_of_lane == head_of_chan).astype(BF16),
        "sc_conv_w": sc_conv_w[l],
        "q_norm_g": att_q_norm_g[l][None, :], "kv_norm_g": att_kv_norm_g[l][None, :],
        "k_norm_g": _pad_cols(idx_k_norm_g[l][None, :], D_IDX, SM_KIDX, LANES),
        "w_qidx": _pad_cols(wqi, D_IDX, SM_KIDX, LANES).reshape(Q_LORA, N_IDX_HEADS * LANES).astype(BF16),
        "w_uq": _pad_cols(wuq, ATT_HEAD_DIM, 0, LANES).reshape(Q_LORA, N_ATT_HEADS * LANES).astype(BF16),
        "w_uk": jnp.pad(att_w_uk[l], ((0, 0), (0, LANES - ATT_HEAD_DIM), (0, 0))).astype(BF16),
        "w_uv": wuv_pad.astype(BF16),
        "ssm_w_out": ssm_w_out[l].astype(BF16), "sc_w_out": sc_w_out[l].astype(BF16),
        "att_w_out": att_w_out[l].astype(BF16), "mix_w_out": mix_w_out[l].astype(BF16),
        "norm2_g": norm2_g[l][None, :],
        "w_router": _pad_cols(jnp.concatenate([moe_w_group[l], moe_w_expert[l]], axis=1),
                              N_EXPERT_GROUPS + N_EXPERTS, 0, LANES),
        "b_router": _pad_cols(jnp.concatenate([moe_b_group[l], moe_b_expert[l]])[None, :],
                              N_EXPERT_GROUPS + N_EXPERTS, 0, LANES),
        "w_gate": moe_w_gate[l].astype(BF16), "w_up": moe_w_up[l].astype(BF16), "w_down": moe_w_down[l].astype(BF16),
    }


def kernel(x, norm1_g, w_in, ssm_conv_w, ssm_conv_b, ssm_dt_bias, ssm_a_log, ssm_d, ssm_norm_g, ssm_w_out, sc_conv_w, sc_w_out, att_q_norm_g, att_kv_norm_g, idx_k_norm_g, att_w_uq, idx_w_q, att_w_uk, att_w_uv, att_w_out, mix_w_out, norm2_g, moe_w_group, moe_b_group, moe_w_expert, moe_b_expert, moe_w_gate, moe_w_up, moe_w_down, final_norm_g):
    b, s, d = x.shape
    depth = w_in.shape[0]
    assert d == D_MODEL and s % DSA_TK == 0 or s < DSA_TK
    x2 = x.reshape(b * s, d)
    final_g = final_norm_g[None, :]
    for l in range(depth):
        p = _layer_params(l, norm1_g, w_in, ssm_conv_w, ssm_conv_b, ssm_dt_bias, ssm_a_log, ssm_d, ssm_norm_g,
                          ssm_w_out, sc_conv_w, sc_w_out, att_q_norm_g, att_kv_norm_g, idx_k_norm_g, att_w_uq,
                          idx_w_q, att_w_uk, att_w_uv, att_w_out, mix_w_out, norm2_g, moe_w_group, moe_b_group,
                          moe_w_expert, moe_b_expert, moe_w_gate, moe_w_up, moe_w_down)
        proj, small = _in_proj(x2, p["norm1_g"], p["w_perm"])
        dtt = jnp.swapaxes(small[:, SM_DT:SM_DT + N_SSM_HEADS].reshape(b, s, N_SSM_HEADS), 1, 2)
        ya = _ssd(proj, small, dtt, p, b, s)
        yb = _sconv(proj, p["sc_conv_w"], b, s)
        qi, ql, ckvn, kidx = _dsa_prep(proj, small, p, b, s)
        yc = _dsa(qi, ql, small, kidx, ckvn, p["w_uv"], b, s)
        x2 = _merge(x2, ya, yb, yc, proj, p)
        x2 = _moe(x2, p, final_g, final_norm=(l == depth - 1))
    return x2.reshape(b, s, d)
```

```python
import functools

import jax
import jax.numpy as jnp
from jax import lax
from jax.experimental import pallas as pl
from jax.experimental.pallas import tpu as pltpu

F32 = jnp.float32
BF16 = jnp.bfloat16
I32 = jnp.int32

EPS = 1e-6
LANES = 128
N_SSM_HEADS = 32
SSM_HEAD_DIM = 64
N_SSM_GROUPS = 4
D_STATE = 128
SSM_CONV = 4
SSM_CHUNK = 128
SC_CONV = 3
N_ATT_HEADS = 16
ATT_HEAD_DIM = 64
ATT_V_DIM = 64
Q_LORA = 384
KV_LORA = 256
N_IDX_HEADS = 8
D_IDX = 64
TOPK_MAX = 256
Q_BLOCK = 128
ATT_SCALE = ATT_HEAD_DIM ** -0.5
LOG2_E = 1.4426950408889634
IDX_SCALE = (N_IDX_HEADS * D_IDX) ** -0.5
N_EXPERT_GROUPS = 4
EXPERTS_PER_GROUP = 4
N_EXPERTS = 16
D_EXPERT = 512

D_MODEL = 1024
D_INNER = 2048
COL_Z = 0
COL_XS = 2048
COL_BM = 4096
COL_CM = 4608
COL_SCB = 5120
COL_SCC = 6144
COL_SCX = 7168
COL_GA = 8192
COL_GB = 9216
COL_GC = 10240
COL_CKV = 11264
COL_CQ = 11520
COL_SMALL = 11904
N_PROJ = 12288
SM_DT = 0
SM_WIDX = 32
SM_KIDX = 64

PROJ_TM = 1024
PROJ_TN = 1024
DSA_TK = 512
ATT_QUARTER_HEADS = 4
ATT_ROW_BLOCK = 16
TOP_PER_LANE = 12
CAND_ROWS = 16
KEY_MIN = -2 ** 31
NEG_BIG = -2.0 ** 100
VMEM_LIMIT = 60 * 1024 * 1024

_NT = (((1,), (1,)), ((), ()))


def _cparams(sem):
    return pltpu.CompilerParams(dimension_semantics=sem, vmem_limit_bytes=VMEM_LIMIT)


def _silu(v):
    return v * jax.nn.sigmoid(v)


def _softplus(v):
    return jnp.maximum(v, 0.0) + jnp.log(1.0 + jnp.exp(-jnp.abs(v)))


def _in_proj_kernel(x_ref, g_ref, w_ref, o_ref, small_ref, hn_ref, *, small_j, small_off):
    j = pl.program_id(1)

    @pl.when(j == 0)
    def _():
        x = x_ref[...]
        hn = x * lax.rsqrt(jnp.mean(x * x, axis=-1, keepdims=True) + EPS) * g_ref[...]
        hn_ref[...] = hn.astype(BF16)

    acc = jnp.dot(hn_ref[...], w_ref[...], preferred_element_type=F32)
    o_ref[...] = acc.astype(BF16)

    @pl.when(j == small_j)
    def _():
        small_ref[...] = acc[:, small_off:small_off + LANES]


def _in_proj(x2, g, w_perm):
    t, d = x2.shape
    tm = min(PROJ_TM, t)
    small_j, small_off = COL_SMALL // PROJ_TN, COL_SMALL % PROJ_TN
    return pl.pallas_call(
        functools.partial(_in_proj_kernel, small_j=small_j, small_off=small_off),
        grid=(t // tm, N_PROJ // PROJ_TN),
        in_specs=[pl.BlockSpec((tm, d), lambda i, j: (i, 0)),
                  pl.BlockSpec((1, d), lambda i, j: (0, 0)),
                  pl.BlockSpec((d, PROJ_TN), lambda i, j: (0, j))],
        out_specs=[pl.BlockSpec((tm, PROJ_TN), lambda i, j: (i, j)),
                   pl.BlockSpec((tm, LANES), lambda i, j: (i, 0))],
        out_shape=[jax.ShapeDtypeStruct((t, N_PROJ), BF16), jax.ShapeDtypeStruct((t, LANES), F32)],
        scratch_shapes=[pltpu.VMEM((tm, d), BF16)],
        compiler_params=_cparams(("parallel", "arbitrary")),
        name="in_proj",
    )(x2, g, w_perm)


def _split_dot(a, e_bf16):
    hi = a.astype(BF16)
    lo = (a - hi.astype(F32)).astype(BF16)
    return (jnp.dot(hi, e_bf16, preferred_element_type=F32) + jnp.dot(lo, e_bf16, preferred_element_type=F32))


def _ssd_kernel(z_ref, xs_ref, bm_ref, cm_ref, small_ref, dtt_ref, convw_ref, convb_ref, dtb_ref, dtbt_ref,
                alog_ref, alogt_ref, dskip_ref, ng_ref, e_ref, o_ref, buf_ref, xbc_ref, state_ref, y_ref):
    c = pl.program_id(1)
    L = SSM_CHUNK
    GW = D_INNER // N_SSM_GROUPS
    HALO = 8

    @pl.when(c == 0)
    def _():
        buf_ref[0:HALO, :] = jnp.zeros((HALO, buf_ref.shape[1]), F32)
        state_ref[...] = jnp.zeros(state_ref.shape, F32)

    buf_ref[HALO:HALO + L, 0:D_INNER] = xs_ref[...].astype(F32)
    buf_ref[HALO:HALO + L, D_INNER:D_INNER + GW] = bm_ref[...].astype(F32)
    buf_ref[HALO:HALO + L, D_INNER + GW:D_INNER + 2 * GW] = cm_ref[...].astype(F32)
    for cc in range(buf_ref.shape[1] // GW):
        cols = slice(cc * GW, (cc + 1) * GW)
        acc = jnp.broadcast_to(convb_ref[:, cols], (L, GW))
        for j in range(SSM_CONV):
            r0 = HALO - (SSM_CONV - 1) + j
            acc = acc + convw_ref[j:j + 1, cols] * buf_ref[r0:r0 + L, cols]
        xbc_ref[:, cols] = _silu(acc)
    buf_ref[0:HALO, :] = buf_ref[L:L + HALO, :]

    row = lax.broadcasted_iota(I32, (L, L), 0)
    col = lax.broadcasted_iota(I32, (L, L), 1)
    causal = row >= col
    tril = causal.astype(F32)
    triu = (row <= col).astype(F32)
    dt = _softplus(small_ref[...] + dtb_ref[...])
    da = dt * (-jnp.exp(alog_ref[...]))
    cs = jnp.dot(tril, da, precision=lax.Precision.HIGHEST, preferred_element_type=F32)
    dtt = _softplus(dtt_ref[0] + dtbt_ref[...])
    dat = dtt * (-jnp.exp(alogt_ref[...]))
    cst = jnp.dot(dat, triu, precision=lax.Precision.HIGHEST, preferred_element_type=F32)
    cs_last = cs[L - 1:L, :]
    e = e_ref[...]
    dt_e = _split_dot(dt, e)
    dte_e = _split_dot(jnp.exp(cs_last - cs), e)
    eo_e = _split_dot(jnp.exp(cs), e)
    cd_e = _split_dot(jnp.broadcast_to(jnp.exp(cs_last), (8, LANES)), e)[0:1, :]

    lane = lax.broadcasted_iota(I32, (L, LANES), 1)
    lo_half = lane < SSM_HEAD_DIM
    for g in range(N_SSM_GROUPS):
        gc = slice(g * GW, (g + 1) * GW)
        bm_g = xbc_ref[:, D_INNER + g * D_STATE:D_INNER + (g + 1) * D_STATE]
        cm_g = xbc_ref[:, D_INNER + GW + g * D_STATE:D_INNER + GW + (g + 1) * D_STATE]
        cm_b = cm_g.astype(BF16)
        cb = lax.dot_general(cm_b, bm_g.astype(BF16), _NT, preferred_element_type=F32)
        xs_g = xbc_ref[:, gc]
        xdt_g = xs_g * dt_e[:, gc]
        st = state_ref[g]
        y_off = jnp.dot(cm_b, st.astype(BF16), preferred_element_type=F32) * eo_e[:, gc]
        for pr in range(GW // LANES):
            x_pair = xdt_g[:, pr * LANES:(pr + 1) * LANES]
            y_pair = y_off[:, pr * LANES:(pr + 1) * LANES]
            for half in range(2):
                h = g * (GW // SSM_HEAD_DIM) + pr * 2 + half
                seg = cs[:, h:h + 1] - cst[h:h + 1, :]
                dec = jnp.exp(jnp.where(causal, seg, -jnp.inf))
                m = (cb * dec).astype(BF16)
                keep = lo_half if half == 0 else jnp.logical_not(lo_half)
                x_h = jnp.where(keep, x_pair, 0.0).astype(BF16)
                y_pair = y_pair + jnp.dot(m, x_h, preferred_element_type=F32)
            y_ref[:, g * GW + pr * LANES:g * GW + (pr + 1) * LANES] = y_pair
        xd = (xdt_g * dte_e[:, gc]).astype(BF16)
        new = jnp.dot(bm_g.T.astype(BF16), xd, preferred_element_type=F32)
        state_ref[g] = cd_e[:, gc] * st + new
        yg = (y_ref[:, gc] + dskip_ref[:, gc] * xs_g) * _silu(z_ref[:, gc].astype(F32))
        yg = yg * lax.rsqrt(jnp.mean(yg * yg, axis=-1, keepdims=True) + EPS)
        o_ref[:, gc] = (yg * ng_ref[:, gc]).astype(BF16)


def _ssd(proj, small, dtt, p, b, s):
    L = SSM_CHUNK
    nc = s // L
    conv_dim = D_INNER + 2 * N_SSM_GROUPS * D_STATE
    row = lambda bi, ci: bi * nc + ci
    full = lambda shape: pl.BlockSpec(shape, lambda bi, ci: tuple(0 for _ in shape))
    return pl.pallas_call(
        _ssd_kernel,
        grid=(b, nc),
        in_specs=[pl.BlockSpec((L, D_INNER), lambda bi, ci: (row(bi, ci), COL_Z // D_INNER)),
                  pl.BlockSpec((L, D_INNER), lambda bi, ci: (row(bi, ci), COL_XS // D_INNER)),
                  pl.BlockSpec((L, 512), lambda bi, ci: (row(bi, ci), COL_BM // 512)),
                  pl.BlockSpec((L, 512), lambda bi, ci: (row(bi, ci), COL_CM // 512)),
                  pl.BlockSpec((L, LANES), lambda bi, ci: (row(bi, ci), 0)),
                  pl.BlockSpec((1, N_SSM_HEADS, L), lambda bi, ci: (bi, 0, ci)),
                  full((SSM_CONV, conv_dim)), full((1, conv_dim)),
                  full((1, LANES)), full((N_SSM_HEADS, 1)), full((1, LANES)), full((N_SSM_HEADS, 1)),
                  full((1, D_INNER)), full((1, D_INNER)), full((LANES, D_INNER))],
        out_specs=pl.BlockSpec((L, D_INNER), lambda bi, ci: (row(bi, ci), 0)),
        out_shape=jax.ShapeDtypeStruct((b * s, D_INNER), BF16),
        scratch_shapes=[pltpu.VMEM((L + 8, conv_dim), F32), pltpu.VMEM((L, conv_dim), F32),
                        pltpu.VMEM((N_SSM_GROUPS, D_STATE, D_INNER // N_SSM_GROUPS), F32),
                        pltpu.VMEM((L, D_INNER), F32)],
        compiler_params=_cparams(("parallel", "arbitrary")),
        name="ssd",
    )(proj, proj, proj, proj, small, dtt, p["conv_w"], p["conv_b"], p["dt_b"], p["dt_bt"], p["a_log"], p["a_logt"],
      p["d_skip"], p["norm_g"], p["expand"])


def _sconv_kernel(b_ref, c_ref, x_ref, w_ref, o_ref, buf_ref):
    i = pl.program_id(1)
    tq = b_ref.shape[0]
    HALO = 8

    @pl.when(i == 0)
    def _():
        buf_ref[0:HALO, :] = jnp.zeros((HALO, buf_ref.shape[1]), F32)

    buf_ref[HALO:HALO + tq, :] = c_ref[...].astype(F32) * x_ref[...].astype(F32)
    v = jnp.zeros((tq, buf_ref.shape[1]), F32)
    for j in range(SC_CONV):
        r0 = HALO - (SC_CONV - 1) + j
        v = v + w_ref[j:j + 1, :] * buf_ref[r0:r0 + tq, :]
    o_ref[...] = (b_ref[...].astype(F32) * v).astype(BF16)
    buf_ref[0:HALO, :] = buf_ref[tq:tq + HALO, :]


def _sconv(proj, w, b, s):
    tq = min(512, s)
    nq = s // tq
    d = D_MODEL
    row = lambda bi, i: bi * nq + i
    return pl.pallas_call(
        _sconv_kernel,
        grid=(b, nq),
        in_specs=[pl.BlockSpec((tq, d), lambda bi, i: (row(bi, i), COL_SCB // d)),
                  pl.BlockSpec((tq, d), lambda bi, i: (row(bi, i), COL_SCC // d)),
                  pl.BlockSpec((tq, d), lambda bi, i: (row(bi, i), COL_SCX // d)),
                  pl.BlockSpec((SC_CONV, d), lambda bi, i: (0, 0))],
        out_specs=pl.BlockSpec((tq, d), lambda bi, i: (row(bi, i), 0)),
        out_shape=jax.ShapeDtypeStruct((b * s, d), BF16),
        scratch_shapes=[pltpu.VMEM((tq + 8, d), F32)],
        compiler_params=_cparams(("parallel", "arbitrary")),
        name="sconv",
    )(proj, proj, proj, w)


def _dsa_prep_kernel(ckv_ref, cq_ref, small_ref, qg_ref, kvg_ref, kg_ref, wqi_ref, wuq_ref, wuk_ref,
                     qi_ref, ql_ref, ckvn_ref, kidx_ref):
    def norm(v, g):
        return v * lax.rsqrt(jnp.mean(v * v, axis=-1, keepdims=True) + EPS) * g

    cq = norm(cq_ref[...].astype(F32), qg_ref[...]).astype(BF16)
    ckvn_ref[...] = norm(ckv_ref[...].astype(F32), kvg_ref[...]).astype(BF16)
    sm = small_ref[...]
    lane = lax.broadcasted_iota(I32, sm.shape, 1)
    ksq = jnp.where(lane >= SM_KIDX, sm * sm, 0.0)
    kidx_ref[...] = (sm * lax.rsqrt(jnp.sum(ksq, axis=-1, keepdims=True) * (1.0 / D_IDX) + EPS) * kg_ref[...]).astype(BF16)
    qi = jnp.dot(cq, wqi_ref[...], preferred_element_type=F32)
    for h in range(N_IDX_HEADS):
        qi_ref[0, h] = qi[:, h * LANES:(h + 1) * LANES].astype(BF16)
    q = jnp.dot(cq, wuq_ref[...], preferred_element_type=F32).astype(BF16)
    for h in range(N_ATT_HEADS):
        ql = jnp.dot(q[:, h * LANES:(h + 1) * LANES], wuk_ref[h], preferred_element_type=F32)
        ql_ref[0, h] = (ql * (ATT_SCALE * LOG2_E)).astype(BF16)


def _dsa_prep(proj, small, p, b, s):
    tm = min(512, s)
    nq = s // tm
    row = lambda bi, i: bi * nq + i
    full = lambda shape: pl.BlockSpec(shape, lambda bi, i: tuple(0 for _ in shape))
    return pl.pallas_call(
        _dsa_prep_kernel,
        grid=(b, nq),
        in_specs=[pl.BlockSpec((tm, KV_LORA), lambda bi, i: (row(bi, i), COL_CKV // KV_LORA)),
                  pl.BlockSpec((tm, Q_LORA), lambda bi, i: (row(bi, i), COL_CQ // Q_LORA)),
                  pl.BlockSpec((tm, LANES), lambda bi, i: (row(bi, i), 0)),
                  full((1, Q_LORA)), full((1, KV_LORA)), full((1, LANES)),
                  full((Q_LORA, N_IDX_HEADS * LANES)), full((Q_LORA, N_ATT_HEADS * LANES)),
                  full((N_ATT_HEADS, LANES, KV_LORA))],
        out_specs=[pl.BlockSpec((1, N_IDX_HEADS, tm, LANES), lambda bi, i: (bi, 0, i, 0)),
                   pl.BlockSpec((1, N_ATT_HEADS, tm, KV_LORA), lambda bi, i: (bi, 0, i, 0)),
                   pl.BlockSpec((tm, KV_LORA), lambda bi, i: (row(bi, i), 0)),
                   pl.BlockSpec((tm, LANES), lambda bi, i: (row(bi, i), 0))],
        out_shape=[jax.ShapeDtypeStruct((b, N_IDX_HEADS, s, LANES), BF16),
                   jax.ShapeDtypeStruct((b, N_ATT_HEADS, s, KV_LORA), BF16),
                   jax.ShapeDtypeStruct((b * s, KV_LORA), BF16),
                   jax.ShapeDtypeStruct((b * s, LANES), BF16)],
        compiler_params=_cparams(("parallel", "parallel")),
        name="dsa_prep",
    )(proj, proj, small, p["q_norm_g"], p["kv_norm_g"], p["k_norm_g"], p["w_qidx"], p["w_uq"], p["w_uk"])


def _dsa_kernel(qi_ref, ql_ref, small_ref, kidx_ref, ckv_ref, wuv_ref, o_ref,
                key_ref, whb_ref, d_ref, s_ref, sb_ref, p_ref, bias_ref,
                m_ref, l_ref, rmax_ref, acc_ref, cand_ref, *, topk, seq_bits):
    qb = pl.program_id(1)
    QB = Q_BLOCK
    TK = key_ref.shape[2]
    NT = TK // LANES
    nkc = (qb * QB + QB + TK - 1) // TK
    q_pos = qb * QB + lax.broadcasted_iota(I32, (QB, 1), 0)
    lane_tk = lax.broadcasted_iota(I32, (QB, TK), 1)

    sm = small_ref[...]
    for h in range(N_IDX_HEADS):
        whb_ref[h] = jnp.broadcast_to(sm[:, SM_WIDX + h:SM_WIDX + h + 1] * IDX_SCALE, (QB, LANES))
    qi = qi_ref[0].reshape(N_IDX_HEADS * QB, LANES)

    def score_chunks(kcs):
        offs = [pl.multiple_of(kc * TK, TK) for kc in kcs]
        for j in range(len(kcs)):
            kx = kidx_ref[pl.ds(offs[j], TK), :]
            d_ref[j] = lax.dot_general(qi, kx, _NT, preferred_element_type=F32)
        for j, kc in enumerate(kcs):
            for t in range(NT):
                sc = jnp.zeros((QB, LANES), F32)
                for h in range(N_IDX_HEADS):
                    d = d_ref[j, h * QB:(h + 1) * QB, t * LANES:(t + 1) * LANES]
                    sc = sc + jnp.maximum(d, 0.0) * whb_ref[h]
                bits = pltpu.bitcast(sc, I32)
                key = bits ^ ((bits >> 31) & 0x7FFFFFFF)
                kpos = offs[j] + t * LANES + lax.broadcasted_iota(I32, (QB, LANES), 1)
                key_ref[kc, :, t * LANES:(t + 1) * LANES] = jnp.where(kpos <= q_pos, key, KEY_MIN)

    def score_pair(i, carry):
        score_chunks([2 * i, 2 * i + 1])
        return carry

    lax.fori_loop(0, nkc // 2, score_pair, 0)

    @pl.when(nkc % 2 == 1)
    def _():
        score_chunks([nkc - 1])

    kk = jnp.minimum(topk, q_pos + 1).astype(F32)

    def count(pred):
        def body(kc, acc):
            k = key_ref[kc]
            for t in range(NT):
                acc = acc + jnp.where(pred(k[:, t * LANES:(t + 1) * LANES], kc * TK + t * LANES), 1.0, 0.0)
            return acc
        acc = lax.fori_loop(0, nkc, body, jnp.zeros((QB, LANES), F32))
        return jnp.sum(acc, axis=-1, keepdims=True)

    def value_bit(i, u):
        cand_u = u | jnp.left_shift(jnp.int32(1), 31 - i)
        cand = cand_u ^ KEY_MIN
        cnt = count(lambda k, base: k >= cand)
        return jnp.where(cnt >= kk, cand_u, u)

    def full_select():
        t = lax.fori_loop(0, 32, value_bit, jnp.zeros((QB, 1), I32)) ^ KEY_MIN
        return t, count(lambda k, base: k > t), count(lambda k, base: k >= t)

    RG = CAND_ROWS

    def collect(rg, carry):
        r0 = pl.multiple_of(rg * RG, RG)

        def insert(kc, tops):
            tops = list(tops)
            for t in range(NT):
                x = key_ref[kc, pl.ds(r0, RG), t * LANES:(t + 1) * LANES]
                for i in range(TOP_PER_LANE):
                    keep = tops[i] >= x
                    tops[i], x = jnp.where(keep, tops[i], x), jnp.where(keep, x, tops[i])
            return tuple(tops)

        tops = lax.fori_loop(0, nkc, insert, tuple(jnp.full((RG, LANES), KEY_MIN, I32) for _ in range(TOP_PER_LANE)))
        for i in range(TOP_PER_LANE):
            cand_ref[i, pl.ds(r0, RG), :] = tops[i]
        return carry

    lax.fori_loop(0, QB // RG, collect, 0)

    def count_cand(pred):
        acc = jnp.zeros((QB, LANES), F32)
        for i in range(TOP_PER_LANE):
            acc = acc + jnp.where(pred(cand_ref[i]), 1.0, 0.0)
        return jnp.sum(acc, axis=-1, keepdims=True)

    def cand_bit(i, u):
        cand_u = u | jnp.left_shift(jnp.int32(1), 31 - i)
        cand = cand_u ^ KEY_MIN
        return jnp.where(count_cand(lambda k: k >= cand) >= kk, cand_u, u)

    thr_c = lax.fori_loop(0, 32, cand_bit, jnp.zeros((QB, 1), I32)) ^ KEY_MIN
    lane_full = jnp.max(jnp.where(cand_ref[TOP_PER_LANE - 1] >= thr_c, 1.0, 0.0)) > 0.0
    thr, c_gt, c_ge = lax.cond(
        lane_full, full_select,
        lambda: (thr_c, count_cand(lambda k: k > thr_c), count_cand(lambda k: k >= thr_c)))
    need = kk - c_gt
    lane128 = lax.broadcasted_iota(I32, (QB, LANES), 1)

    def tie_cut():
        def index_bit(i, pcut):
            cand = pcut | jnp.left_shift(jnp.int32(1), seq_bits - 1 - i)
            cnt = count(lambda k, base: (k == thr) & (base + lane128 < cand))
            return jnp.where(cnt < need, cand, pcut)
        return lax.fori_loop(0, seq_bits, index_bit, jnp.zeros((QB, 1), I32))

    has_ties = jnp.max(c_ge - kk) > 0.0
    pcut = lax.cond(has_ties, tie_cut, lambda: jnp.full((QB, 1), 2 ** seq_bits - 1, I32))

    m_ref[...] = jnp.full(m_ref.shape, NEG_BIG, F32)
    l_ref[...] = jnp.zeros(l_ref.shape, F32)
    acc_ref[...] = jnp.zeros(acc_ref.shape, F32)
    HQ = ATT_QUARTER_HEADS
    n_quarters = N_ATT_HEADS // HQ
    RB = ATT_ROW_BLOCK
    n_rb = QB // RB

    def attend(kcs):
        offs = [pl.multiple_of(kc * TK, TK) for kc in kcs]
        for j, kc in enumerate(kcs):
            k = key_ref[kc]
            sel = (k > thr) | ((k == thr) & (offs[j] + lane_tk <= pcut))
            bias_ref[j] = jnp.where(sel, 0.0, NEG_BIG).astype(BF16)

        def logits(j, qt):
            ql = ql_ref[0, qt * HQ:(qt + 1) * HQ].reshape(HQ * QB, KV_LORA)
            kv = ckv_ref[pl.ds(offs[j], TK), :]
            s_ref[qt] = lax.dot_general(ql, kv, _NT, preferred_element_type=F32)

        def row_max(j, qt, rb):
            bias = bias_ref[j, rb * RB:(rb + 1) * RB, :]
            for hh in range(HQ):
                r0 = hh * QB + rb * RB
                s = s_ref[qt, r0:r0 + RB, :].astype(BF16) + bias
                sb_ref[qt, r0:r0 + RB, :] = s
                mx = s[:, 0:LANES]
                for t in range(1, NT):
                    mx = jnp.maximum(mx, s[:, t * LANES:(t + 1) * LANES])
                g0 = qt * HQ * QB + r0
                rmax_ref[g0:g0 + RB, :] = jnp.broadcast_to(
                    jnp.max(mx.astype(F32), axis=-1, keepdims=True), (RB, LANES))

        def row_probs(qt, rb):
            for hh in range(HQ):
                r0 = hh * QB + rb * RB
                g0 = qt * HQ * QB + r0
                m_prev = m_ref[g0:g0 + RB, :]
                m_new = jnp.maximum(m_prev, rmax_ref[g0:g0 + RB, :])
                alpha = jnp.exp2(m_prev - m_new)
                m_b = m_new.astype(BF16)
                psum = None
                for t in range(NT):
                    cols = slice(t * LANES, (t + 1) * LANES)
                    p = jnp.exp2(sb_ref[qt, r0:r0 + RB, cols] - m_b)
                    psum = p if psum is None else psum + p
                    p_ref[qt, r0:r0 + RB, cols] = p
                l_ref[g0:g0 + RB, :] = alpha * l_ref[g0:g0 + RB, :] + psum.astype(F32)
                m_ref[g0:g0 + RB, :] = m_new
                for c in range(KV_LORA // LANES):
                    cols = slice(c * LANES, (c + 1) * LANES)
                    acc_ref[g0:g0 + RB, cols] = acc_ref[g0:g0 + RB, cols] * alpha

        items = [(j, qt) for j in range(len(kcs)) for qt in range(n_quarters)]
        logits(*items[0])
        for i, (j, qt) in enumerate(items):
            if i + 1 < len(items):
                logits(*items[i + 1])
            for rb in range(n_rb):
                row_max(j, qt, rb)
            for rb in range(n_rb):
                row_probs(qt, rb)
            rows = slice(qt * HQ * QB, (qt + 1) * HQ * QB)
            kv = ckv_ref[pl.ds(offs[j], TK), :]
            acc_ref[rows, :] += jnp.dot(p_ref[qt], kv, preferred_element_type=F32)

    def attend_pair(i, carry):
        attend([2 * i, 2 * i + 1])
        return carry

    lax.fori_loop(0, nkc // 2, attend_pair, 0)

    @pl.when(nkc % 2 == 1)
    def _():
        attend([nkc - 1])

    for pr in range(N_ATT_HEADS // 2):
        out = jnp.zeros((QB, LANES), F32)
        for half in range(2):
            h = 2 * pr + half
            rows = slice(h * QB, (h + 1) * QB)
            l_row = jnp.sum(l_ref[rows, :], axis=-1, keepdims=True)
            o_lat = (acc_ref[rows, :] / l_row).astype(BF16)
            out = out + jnp.dot(o_lat, wuv_ref[h], preferred_element_type=F32)
        o_ref[:, pr * LANES:(pr + 1) * LANES] = out.astype(BF16)


def _dsa(qi, ql, small, kidx, ckvn, wuv, b, s):
    QB = Q_BLOCK
    TK = min(DSA_TK, s)
    nq = s // QB
    topk = min(TOPK_MAX, s // 4)
    seq_bits = (s - 1).bit_length()
    resident = dict(pipeline_mode=pl.Buffered(1))
    return pl.pallas_call(
        functools.partial(_dsa_kernel, topk=topk, seq_bits=seq_bits),
        grid=(b, nq),
        in_specs=[pl.BlockSpec((1, N_IDX_HEADS, QB, LANES), lambda bi, i: (bi, 0, i, 0)),
                  pl.BlockSpec((1, N_ATT_HEADS, QB, KV_LORA), lambda bi, i: (bi, 0, i, 0)),
                  pl.BlockSpec((QB, LANES), lambda bi, i: (bi * nq + i, 0)),
                  pl.BlockSpec((s, LANES), lambda bi, i: (bi, 0), **resident),
                  pl.BlockSpec((s, KV_LORA), lambda bi, i: (bi, 0), **resident),
                  pl.BlockSpec((N_ATT_HEADS, KV_LORA, LANES), lambda bi, i: (0, 0, 0))],
        out_specs=pl.BlockSpec((QB, N_ATT_HEADS * ATT_V_DIM), lambda bi, i: (bi * nq + i, 0)),
        out_shape=jax.ShapeDtypeStruct((b * s, N_ATT_HEADS * ATT_V_DIM), BF16),
        scratch_shapes=[pltpu.VMEM((s // TK, QB, TK), I32),
                        pltpu.VMEM((N_IDX_HEADS, QB, LANES), F32),
                        pltpu.VMEM((2, N_IDX_HEADS * QB, TK), F32),
                        pltpu.VMEM((N_ATT_HEADS // ATT_QUARTER_HEADS, ATT_QUARTER_HEADS * QB, TK), F32),
                        pltpu.VMEM((N_ATT_HEADS // ATT_QUARTER_HEADS, ATT_QUARTER_HEADS * QB, TK), BF16),
                        pltpu.VMEM((N_ATT_HEADS // ATT_QUARTER_HEADS, ATT_QUARTER_HEADS * QB, TK), BF16),
                        pltpu.VMEM((2, QB, TK), BF16),
                        pltpu.VMEM((N_ATT_HEADS * QB, LANES), F32),
                        pltpu.VMEM((N_ATT_HEADS * QB, LANES), F32),
                        pltpu.VMEM((N_ATT_HEADS * QB, LANES), F32),
                        pltpu.VMEM((N_ATT_HEADS * QB, KV_LORA), F32),
                        pltpu.VMEM((TOP_PER_LANE, QB, LANES), I32)],
        compiler_params=_cparams(("parallel", "arbitrary")),
        name="dsa",
    )(qi, ql, small, kidx, ckvn, wuv)


def _merge_kernel(x_ref, ya_ref, yb_ref, yc_ref, ga_ref, gb_ref, gc_ref, wa_ref, wb_ref, wc_ref, wm_ref, o_ref):
    def branch(y_ref, w_ref, g_ref):
        y = jnp.dot(y_ref[...], w_ref[...], preferred_element_type=F32)
        return jax.nn.sigmoid(g_ref[...].astype(F32)) * y

    merged = branch(ya_ref, wa_ref, ga_ref) + branch(yb_ref, wb_ref, gb_ref) + branch(yc_ref, wc_ref, gc_ref)
    o_ref[...] = x_ref[...] + jnp.dot(merged.astype(BF16), wm_ref[...], preferred_element_type=F32)


def _merge(x2, ya, yb, yc, proj, p):
    t, d = x2.shape
    tm = min(512, t)
    tok = lambda cols, cb: pl.BlockSpec((tm, cols), lambda i: (i, cb))
    full = lambda shape: pl.BlockSpec(shape, lambda i: (0, 0))
    return pl.pallas_call(
        _merge_kernel,
        grid=(t // tm,),
        in_specs=[tok(d, 0), tok(D_INNER, 0), tok(d, 0), tok(d, 0),
                  tok(d, COL_GA // d), tok(d, COL_GB // d), tok(d, COL_GC // d),
                  full((D_INNER, d)), full((d, d)), full((d, d)), full((d, d))],
        out_specs=tok(d, 0),
        out_shape=jax.ShapeDtypeStruct((t, d), F32),
        compiler_params=_cparams(("parallel",)),
        name="merge",
    )(x2, ya, yb, yc, proj, proj, proj, p["ssm_w_out"], p["sc_w_out"], p["att_w_out"], p["mix_w_out"])


def _first_argmax(v, lane):
    m = jnp.max(v, axis=-1, keepdims=True)
    idx = jnp.min(jnp.where(v == m, lane, LANES), axis=-1, keepdims=True)
    return m, idx


def _moe_kernel(x_ref, g_ref, wr_ref, br_ref, wg_ref, wu_ref, wd_ref, fg_ref, o_ref, h_ref, comb_ref, acc_ref,
                *, final_norm):
    e = pl.program_id(1)

    @pl.when(e == 0)
    def _():
        x = x_ref[...]
        h = x * lax.rsqrt(jnp.mean(x * x, axis=-1, keepdims=True) + EPS) * g_ref[...]
        h_ref[...] = h.astype(BF16)
        logits = jnp.dot(h, wr_ref[...], precision=lax.Precision.HIGHEST, preferred_element_type=F32) + br_ref[...]
        lane = lax.broadcasted_iota(I32, logits.shape, 1)
        neg_inf = -jnp.inf
        lg = jnp.where(lane < N_EXPERT_GROUPS, logits, neg_inf)
        eg = jnp.exp(lg - jnp.max(lg, axis=-1, keepdims=True))
        pg = eg / jnp.sum(eg, axis=-1, keepdims=True)
        p_group, g_sel = _first_argmax(jnp.where(lane < N_EXPERT_GROUPS, pg, neg_inf), lane)
        lo = N_EXPERT_GROUPS + g_sel * EXPERTS_PER_GROUP
        in_group = (lane >= lo) & (lane < lo + EXPERTS_PER_GROUP)
        le = jnp.where(in_group, logits, neg_inf)
        ee = jnp.exp(le - jnp.max(le, axis=-1, keepdims=True))
        pe = jnp.where(in_group, ee / jnp.sum(ee, axis=-1, keepdims=True), neg_inf)
        p1, i1 = _first_argmax(pe, lane)
        p2, i2 = _first_argmax(jnp.where(lane == i1, neg_inf, pe), lane)
        tot = p1 + p2
        comb_ref[...] = (jnp.where(lane == i1, p_group * (p1 / tot), 0.0)
                         + jnp.where(lane == i2, p_group * (p2 / tot), 0.0))
        acc_ref[...] = jnp.zeros(acc_ref.shape, F32)

    h = h_ref[...]
    lane = lax.broadcasted_iota(I32, comb_ref.shape, 1)
    c_e = jnp.sum(jnp.where(lane == e + N_EXPERT_GROUPS, comb_ref[...], 0.0), axis=-1, keepdims=True)
    gate = jnp.dot(h, wg_ref[0], preferred_element_type=F32)
    up = jnp.dot(h, wu_ref[0], preferred_element_type=F32)
    hid = (_silu(gate) * up * c_e).astype(BF16)
    acc_ref[...] += jnp.dot(hid, wd_ref[0], preferred_element_type=F32)

    @pl.when(e == N_EXPERTS - 1)
    def _():
        y = x_ref[...] + acc_ref[...]
        if final_norm:
            y = y * lax.rsqrt(jnp.mean(y * y, axis=-1, keepdims=True) + EPS) * fg_ref[...]
        o_ref[...] = y


def _moe(x2, p, final_g, final_norm):
    t, d = x2.shape
    tm = min(1024, t)
    full = lambda shape: pl.BlockSpec(shape, lambda i, e: (0, 0))
    return pl.pallas_call(
        functools.partial(_moe_kernel, final_norm=final_norm),
        grid=(t // tm, N_EXPERTS),
        in_specs=[pl.BlockSpec((tm, d), lambda i, e: (i, 0)), full((1, d)), full((d, LANES)), full((1, LANES)),
                  pl.BlockSpec((1, d, D_EXPERT), lambda i, e: (e, 0, 0)),
                  pl.BlockSpec((1, d, D_EXPERT), lambda i, e: (e, 0, 0)),
                  pl.BlockSpec((1, D_EXPERT, d), lambda i, e: (e, 0, 0)),
                  full((1, d))],
        out_specs=pl.BlockSpec((tm, d), lambda i, e: (i, 0)),
        out_shape=jax.ShapeDtypeStruct((t, d), F32),
        scratch_shapes=[pltpu.VMEM((tm, d), BF16), pltpu.VMEM((tm, LANES), F32), pltpu.VMEM((tm, d), F32)],
        compiler_params=_cparams(("parallel", "arbitrary")),
        name="moe",
    )(x2, p["norm2_g"], p["w_router"], p["b_router"], p["w_gate"], p["w_up"], p["w_down"], final_g)


def _pad_cols(w, width, offset, total):
    return jnp.pad(w, [(0, 0)] * (w.ndim - 1) + [(offset, total - offset - width)])


def _layer_params(l, norm1_g, w_in, ssm_conv_w, ssm_conv_b, ssm_dt_bias, ssm_a_log, ssm_d, ssm_norm_g, ssm_w_out,
                  sc_conv_w, sc_w_out, att_q_norm_g, att_kv_norm_g, idx_k_norm_g, att_w_uq, idx_w_q, att_w_uk,
                  att_w_uv, att_w_out, mix_w_out, norm2_g, moe_w_group, moe_b_group, moe_w_expert, moe_b_expert,
                  moe_w_gate, moe_w_up, moe_w_down):
    w = w_in[l]
    o = 0
    seg = {}
    for name, size in (("z", 2048), ("xbc", 3072), ("dt", 32), ("scb", 1024), ("scc", 1024), ("scx", 1024),
                       ("cq", Q_LORA), ("ckv", KV_LORA), ("kidx", D_IDX), ("widx", N_IDX_HEADS),
                       ("ga", 1024), ("gb", 1024), ("gc", 1024)):
        seg[name] = w[:, o:o + size]
        o += size
    d = w.shape[0]
    small = jnp.concatenate([seg["dt"], seg["widx"], jnp.zeros((d, SM_KIDX - SM_WIDX - N_IDX_HEADS), F32),
                             seg["kidx"]], axis=1)
    w_perm = jnp.concatenate([seg["z"], seg["xbc"], seg["scb"], seg["scc"], seg["scx"], seg["ga"], seg["gb"],
                              seg["gc"], seg["ckv"], seg["cq"], small,
                              jnp.zeros((d, N_PROJ - COL_SMALL - LANES), F32)], axis=1).astype(BF16)
    head_of_lane = jnp.arange(LANES)[:, None]
    head_of_chan = (jnp.arange(D_INNER) // SSM_HEAD_DIM)[None, :]
    wqi = idx_w_q[l]
    wuq = att_w_uq[l]
    wuv = att_w_uv[l]
    wuv_pad = jnp.stack([_pad_cols(wuv[h], ATT_V_DIM, (h % 2) * ATT_V_DIM, LANES) for h in range(N_ATT_HEADS)])
    return {
        "norm1_g": norm1_g[l][None, :], "w_perm": w_perm,
        "conv_w": ssm_conv_w[l], "conv_b": ssm_conv_b[l][None, :],
        "dt_b": _pad_cols(ssm_dt_bias[l][None, :], N_SSM_HEADS, SM_DT, LANES), "dt_bt": ssm_dt_bias[l][:, None],
        "a_log": _pad_cols(ssm_a_log[l][None, :], N_SSM_HEADS, SM_DT, LANES), "a_logt": ssm_a_log[l][:, None],
        "d_skip": jnp.repeat(ssm_d[l], SSM_HEAD_DIM)[None, :], "norm_g": ssm_norm_g[l][None, :],
        "expand": (head_of_lane == head_of_chan).astype(BF16),
        "sc_conv_w": sc_conv_w[l],
        "q_norm_g": att_q_norm_g[l][None, :], "kv_norm_g": att_kv_norm_g[l][None, :],
        "k_norm_g": _pad_cols(idx_k_norm_g[l][None, :], D_IDX, SM_KIDX, LANES),
        "w_qidx": _pad_cols(wqi, D_IDX, SM_KIDX, LANES).reshape(Q_LORA, N_IDX_HEADS * LANES).astype(BF16),
        "w_uq": _pad_cols(wuq, ATT_HEAD_DIM, 0, LANES).reshape(Q_LORA, N_ATT_HEADS * LANES).astype(BF16),
        "w_uk": jnp.pad(att_w_uk[l], ((0, 0), (0, LANES - ATT_HEAD_DIM), (0, 0))).astype(BF16),
        "w_uv": wuv_pad.astype(BF16),
        "ssm_w_out": ssm_w_out[l].astype(BF16), "sc_w_out": sc_w_out[l].astype(BF16),
        "att_w_out": att_w_out[l].astype(BF16), "mix_w_out": mix_w_out[l].astype(BF16),
        "norm2_g": norm2_g[l][None, :],
        "w_router": _pad_cols(jnp.concatenate([moe_w_group[l], moe_w_expert[l]], axis=1),
                              N_EXPERT_GROUPS + N_EXPERTS, 0, LANES),
        "b_router": _pad_cols(jnp.concatenate([moe_b_group[l], moe_b_expert[l]])[None, :],
                              N_EXPERT_GROUPS + N_EXPERTS, 0, LANES),
        "w_gate": moe_w_gate[l].astype(BF16), "w_up": moe_w_up[l].astype(BF16), "w_down": moe_w_down[l].astype(BF16),
    }


def kernel(x, norm1_g, w_in, ssm_conv_w, ssm_conv_b, ssm_dt_bias, ssm_a_log, ssm_d, ssm_norm_g, ssm_w_out, sc_conv_w, sc_w_out, att_q_norm_g, att_kv_norm_g, idx_k_norm_g, att_w_uq, idx_w_q, att_w_uk, att_w_uv, att_w_out, mix_w_out, norm2_g, moe_w_group, moe_b_group, moe_w_expert, moe_b_expert, moe_w_gate, moe_w_up, moe_w_down, final_norm_g):
    b, s, d = x.shape
    depth = w_in.shape[0]
    assert d == D_MODEL and s % DSA_TK == 0 or s < DSA_TK
    x2 = x.reshape(b * s, d)
    final_g = final_norm_g[None, :]
    for l in range(depth):
        p = _layer_params(l, norm1_g, w_in, ssm_conv_w, ssm_conv_b, ssm_dt_bias, ssm_a_log, ssm_d, ssm_norm_g,
                          ssm_w_out, sc_conv_w, sc_w_out, att_q_norm_g, att_kv_norm_g, idx_k_norm_g, att_w_uq,
                          idx_w_q, att_w_uk, att_w_uv, att_w_out, mix_w_out, norm2_g, moe_w_group, moe_b_group,
                          moe_w_expert, moe_b_expert, moe_w_gate, moe_w_up, moe_w_down)
        proj, small = _in_proj(x2, p["norm1_g"], p["w_perm"])
        dtt = jnp.swapaxes(small[:, SM_DT:SM_DT + N_SSM_HEADS].reshape(b, s, N_SSM_HEADS), 1, 2)
        ya = _ssd(proj, small, dtt, p, b, s)
        yb = _sconv(proj, p["sc_conv_w"], b, s)
        qi, ql, ckvn, kidx = _dsa_prep(proj, small, p, b, s)
        yc = _dsa(qi, ql, small, kidx, ckvn, p["w_uv"], b, s)
        x2 = _merge(x2, ya, yb, yc, proj, p)
        x2 = _moe(x2, p, final_g, final_norm=(l == depth - 1))
    return x2.reshape(b, s, d)
```

```python
import functools

import jax
import jax.numpy as jnp
from jax import lax
from jax.experimental import pallas as pl
from jax.experimental.pallas import tpu as pltpu

F32 = jnp.float32
BF16 = jnp.bfloat16
I32 = jnp.int32
ATT_MM_DTYPE = jnp.bfloat16

EPS = 1e-6
LANES = 128
N_SSM_HEADS = 32
SSM_HEAD_DIM = 64
N_SSM_GROUPS = 4
D_STATE = 128
SSM_CONV = 4
SSM_CHUNK = 128
SC_CONV = 3
N_ATT_HEADS = 16
ATT_HEAD_DIM = 64
ATT_V_DIM = 64
Q_LORA = 384
KV_LORA = 256
N_IDX_HEADS = 8
D_IDX = 64
TOPK_MAX = 256
Q_BLOCK = 128
ATT_SCALE = ATT_HEAD_DIM ** -0.5
LOG2_E = 1.4426950408889634
IDX_SCALE = (N_IDX_HEADS * D_IDX) ** -0.5
N_EXPERT_GROUPS = 4
EXPERTS_PER_GROUP = 4
N_EXPERTS = 16
D_EXPERT = 512

D_MODEL = 1024
D_INNER = 2048
COL_Z = 0
COL_XS = 2048
COL_BM = 4096
COL_CM = 4608
COL_SCB = 5120
COL_SCC = 6144
COL_SCX = 7168
COL_GA = 8192
COL_GB = 9216
COL_GC = 10240
COL_CKV = 11264
COL_CQ = 11520
COL_SMALL = 11904
N_PROJ = 12288
SM_DT = 0
SM_WIDX = 32
SM_KIDX = 64

PROJ_TM = 1024
PROJ_TN = 1024
DSA_TK = 512
ATT_QUARTER_HEADS = 4
ATT_ROW_BLOCK = 16
ATT_CHUNKS_PER_STEP = 4
TOP_PER_LANE = 12
CAND_ROWS = 32
KEY_MIN = -2 ** 31
NEG_BIG = -2.0 ** 100
VMEM_LIMIT = 60 * 1024 * 1024

_NT = (((1,), (1,)), ((), ()))


def _cparams(sem):
    return pltpu.CompilerParams(dimension_semantics=sem, vmem_limit_bytes=VMEM_LIMIT)


def _silu(v):
    half = 0.5 * v
    return half + half * jnp.tanh(half)


def _softplus(v):
    return jnp.maximum(v, 0.0) + jnp.log(1.0 + jnp.exp(-jnp.abs(v)))


def _in_proj_kernel(x_ref, g_ref, w_ref, o_ref, small_ref, hn_ref, *, small_j, small_off):
    j = pl.program_id(1)

    @pl.when(j == 0)
    def _():
        x = x_ref[...]
        hn = x * lax.rsqrt(jnp.mean(x * x, axis=-1, keepdims=True) + EPS) * g_ref[...]
        hn_ref[...] = hn.astype(BF16)

    acc = jnp.dot(hn_ref[...], w_ref[...], preferred_element_type=F32)
    o_ref[...] = acc.astype(BF16)

    @pl.when(j == small_j)
    def _():
        small_ref[...] = acc[:, small_off:small_off + LANES]


def _in_proj(x2, g, w_perm):
    t, d = x2.shape
    tm = min(PROJ_TM, t)
    small_j, small_off = COL_SMALL // PROJ_TN, COL_SMALL % PROJ_TN
    return pl.pallas_call(
        functools.partial(_in_proj_kernel, small_j=small_j, small_off=small_off),
        grid=(t // tm, N_PROJ // PROJ_TN),
        in_specs=[pl.BlockSpec((tm, d), lambda i, j: (i, 0)),
                  pl.BlockSpec((1, d), lambda i, j: (0, 0)),
                  pl.BlockSpec((d, PROJ_TN), lambda i, j: (0, j))],
        out_specs=[pl.BlockSpec((tm, PROJ_TN), lambda i, j: (i, j)),
                   pl.BlockSpec((tm, LANES), lambda i, j: (i, 0))],
        out_shape=[jax.ShapeDtypeStruct((t, N_PROJ), BF16), jax.ShapeDtypeStruct((t, LANES), F32)],
        scratch_shapes=[pltpu.VMEM((tm, d), BF16)],
        compiler_params=_cparams(("parallel", "arbitrary")),
        name="in_proj",
    )(x2, g, w_perm)


def _split_dot(a, e_bf16):
    hi = a.astype(BF16)
    lo = (a - hi.astype(F32)).astype(BF16)
    return (jnp.dot(hi, e_bf16, preferred_element_type=F32) + jnp.dot(lo, e_bf16, preferred_element_type=F32))


def _ssd_kernel(z_ref, xs_ref, bm_ref, cm_ref, small_ref, dtt_ref, convw_ref, convb_ref, dtb_ref, dtbt_ref,
                alog_ref, alogt_ref, dskip_ref, ng_ref, e_ref, o_ref, buf_ref, xbc_ref, state_ref, y_ref):
    c = pl.program_id(1)
    L = SSM_CHUNK
    GW = D_INNER // N_SSM_GROUPS
    HALO = 8

    @pl.when(c == 0)
    def _():
        buf_ref[0:HALO, :] = jnp.zeros((HALO, buf_ref.shape[1]), F32)
        state_ref[...] = jnp.zeros(state_ref.shape, F32)

    buf_ref[HALO:HALO + L, 0:D_INNER] = xs_ref[...].astype(F32)
    buf_ref[HALO:HALO + L, D_INNER:D_INNER + GW] = bm_ref[...].astype(F32)
    buf_ref[HALO:HALO + L, D_INNER + GW:D_INNER + 2 * GW] = cm_ref[...].astype(F32)
    for cc in range(buf_ref.shape[1] // GW):
        cols = slice(cc * GW, (cc + 1) * GW)
        acc = jnp.broadcast_to(convb_ref[:, cols], (L, GW))
        for j in range(SSM_CONV):
            r0 = HALO - (SSM_CONV - 1) + j
            acc = acc + convw_ref[j:j + 1, cols] * buf_ref[r0:r0 + L, cols]
        xbc_ref[:, cols] = _silu(acc)
    buf_ref[0:HALO, :] = buf_ref[L:L + HALO, :]

    row = lax.broadcasted_iota(I32, (L, L), 0)
    col = lax.broadcasted_iota(I32, (L, L), 1)
    causal = row >= col
    tril = causal.astype(F32)
    triu = (row <= col).astype(F32)
    dt = _softplus(small_ref[...] + dtb_ref[...])
    da = dt * (-jnp.exp(alog_ref[...]))
    cs = jnp.dot(tril, da, precision=lax.Precision.HIGHEST, preferred_element_type=F32)
    dtt = _softplus(dtt_ref[0] + dtbt_ref[...])
    dat = dtt * (-jnp.exp(alogt_ref[...]))
    cst = jnp.dot(dat, triu, precision=lax.Precision.HIGHEST, preferred_element_type=F32)
    cs_last = cs[L - 1:L, :]
    e = e_ref[...]
    dt_e = _split_dot(dt, e)
    dte_e = _split_dot(jnp.exp(cs_last - cs), e)
    eo_e = _split_dot(jnp.exp(cs), e)
    cd_e = _split_dot(jnp.broadcast_to(jnp.exp(cs_last), (8, LANES)), e)[0:1, :]

    lane = lax.broadcasted_iota(I32, (L, LANES), 1)
    lo_half = lane < SSM_HEAD_DIM
    for g in range(N_SSM_GROUPS):
        gc = slice(g * GW, (g + 1) * GW)
        bm_g = xbc_ref[:, D_INNER + g * D_STATE:D_INNER + (g + 1) * D_STATE]
        cm_g = xbc_ref[:, D_INNER + GW + g * D_STATE:D_INNER + GW + (g + 1) * D_STATE]
        cm_b = cm_g.astype(BF16)
        cb = lax.dot_general(cm_b, bm_g.astype(BF16), _NT, preferred_element_type=F32)
        xs_g = xbc_ref[:, gc]
        xdt_g = xs_g * dt_e[:, gc]
        st = state_ref[g]
        y_off = jnp.dot(cm_b, st.astype(BF16), preferred_element_type=F32) * eo_e[:, gc]
        for pr in range(GW // LANES):
            x_pair = xdt_g[:, pr * LANES:(pr + 1) * LANES]
            y_pair = y_off[:, pr * LANES:(pr + 1) * LANES]
            for half in range(2):
                h = g * (GW // SSM_HEAD_DIM) + pr * 2 + half
                seg = cs[:, h:h + 1] - cst[h:h + 1, :]
                dec = jnp.exp(jnp.where(causal, seg, -jnp.inf))
                m = (cb * dec).astype(BF16)
                keep = lo_half if half == 0 else jnp.logical_not(lo_half)
                x_h = jnp.where(keep, x_pair, 0.0).astype(BF16)
                y_pair = y_pair + jnp.dot(m, x_h, preferred_element_type=F32)
            y_ref[:, g * GW + pr * LANES:g * GW + (pr + 1) * LANES] = y_pair
        xd = (xdt_g * dte_e[:, gc]).astype(BF16)
        new = jnp.dot(bm_g.T.astype(BF16), xd, preferred_element_type=F32)
        state_ref[g] = cd_e[:, gc] * st + new
        yg = (y_ref[:, gc] + dskip_ref[:, gc] * xs_g) * _silu(z_ref[:, gc].astype(F32))
        yg = yg * lax.rsqrt(jnp.mean(yg * yg, axis=-1, keepdims=True) + EPS)
        o_ref[:, gc] = (yg * ng_ref[:, gc]).astype(BF16)


def _ssd(proj, small, dtt, p, b, s):
    L = SSM_CHUNK
    nc = s // L
    conv_dim = D_INNER + 2 * N_SSM_GROUPS * D_STATE
    row = lambda bi, ci: bi * nc + ci
    full = lambda shape: pl.BlockSpec(shape, lambda bi, ci: tuple(0 for _ in shape))
    return pl.pallas_call(
        _ssd_kernel,
        grid=(b, nc),
        in_specs=[pl.BlockSpec((L, D_INNER), lambda bi, ci: (row(bi, ci), COL_Z // D_INNER)),
                  pl.BlockSpec((L, D_INNER), lambda bi, ci: (row(bi, ci), COL_XS // D_INNER)),
                  pl.BlockSpec((L, 512), lambda bi, ci: (row(bi, ci), COL_BM // 512)),
                  pl.BlockSpec((L, 512), lambda bi, ci: (row(bi, ci), COL_CM // 512)),
                  pl.BlockSpec((L, LANES), lambda bi, ci: (row(bi, ci), 0)),
                  pl.BlockSpec((1, N_SSM_HEADS, L), lambda bi, ci: (bi, 0, ci)),
                  full((SSM_CONV, conv_dim)), full((1, conv_dim)),
                  full((1, LANES)), full((N_SSM_HEADS, 1)), full((1, LANES)), full((N_SSM_HEADS, 1)),
                  full((1, D_INNER)), full((1, D_INNER)), full((LANES, D_INNER))],
        out_specs=pl.BlockSpec((L, D_INNER), lambda bi, ci: (row(bi, ci), 0)),
        out_shape=jax.ShapeDtypeStruct((b * s, D_INNER), BF16),
        scratch_shapes=[pltpu.VMEM((L + 8, conv_dim), F32), pltpu.VMEM((L, conv_dim), F32),
                        pltpu.VMEM((N_SSM_GROUPS, D_STATE, D_INNER // N_SSM_GROUPS), F32),
                        pltpu.VMEM((L, D_INNER), F32)],
        compiler_params=_cparams(("parallel", "arbitrary")),
        name="ssd",
    )(proj, proj, proj, proj, small, dtt, p["conv_w"], p["conv_b"], p["dt_b"], p["dt_bt"], p["a_log"], p["a_logt"],
      p["d_skip"], p["norm_g"], p["expand"])


def _sconv_kernel(b_ref, c_ref, x_ref, w_ref, o_ref, buf_ref):
    i = pl.program_id(1)
    tq = b_ref.shape[0]
    HALO = 8

    @pl.when(i == 0)
    def _():
        buf_ref[0:HALO, :] = jnp.zeros((HALO, buf_ref.shape[1]), F32)

    buf_ref[HALO:HALO + tq, :] = c_ref[...].astype(F32) * x_ref[...].astype(F32)
    v = jnp.zeros((tq, buf_ref.shape[1]), F32)
    for j in range(SC_CONV):
        r0 = HALO - (SC_CONV - 1) + j
        v = v + w_ref[j:j + 1, :] * buf_ref[r0:r0 + tq, :]
    o_ref[...] = (b_ref[...].astype(F32) * v).astype(BF16)
    buf_ref[0:HALO, :] = buf_ref[tq:tq + HALO, :]


def _sconv(proj, w, b, s):
    tq = min(512, s)
    nq = s // tq
    d = D_MODEL
    row = lambda bi, i: bi * nq + i
    return pl.pallas_call(
        _sconv_kernel,
        grid=(b, nq),
        in_specs=[pl.BlockSpec((tq, d), lambda bi, i: (row(bi, i), COL_SCB // d)),
                  pl.BlockSpec((tq, d), lambda bi, i: (row(bi, i), COL_SCC // d)),
                  pl.BlockSpec((tq, d), lambda bi, i: (row(bi, i), COL_SCX // d)),
                  pl.BlockSpec((SC_CONV, d), lambda bi, i: (0, 0))],
        out_specs=pl.BlockSpec((tq, d), lambda bi, i: (row(bi, i), 0)),
        out_shape=jax.ShapeDtypeStruct((b * s, d), BF16),
        scratch_shapes=[pltpu.VMEM((tq + 8, d), F32)],
        compiler_params=_cparams(("parallel", "arbitrary")),
        name="sconv",
    )(proj, proj, proj, w)


def _dsa_prep_kernel(ckv_ref, cq_ref, small_ref, qg_ref, kvg_ref, kg_ref, wqi_ref, wuq_ref, wuk_ref,
                     qi_ref, ql_ref, ckvn_ref, kidx_ref):
    def norm(v, g):
        return v * lax.rsqrt(jnp.mean(v * v, axis=-1, keepdims=True) + EPS) * g

    cq = norm(cq_ref[...].astype(F32), qg_ref[...]).astype(BF16)
    ckvn_ref[...] = norm(ckv_ref[...].astype(F32), kvg_ref[...]).astype(ATT_MM_DTYPE)
    sm = small_ref[...]
    lane = lax.broadcasted_iota(I32, sm.shape, 1)
    ksq = jnp.where(lane >= SM_KIDX, sm * sm, 0.0)
    kidx_ref[...] = (sm * lax.rsqrt(jnp.sum(ksq, axis=-1, keepdims=True) * (1.0 / D_IDX) + EPS) * kg_ref[...]).astype(BF16)
    qi = jnp.dot(cq, wqi_ref[...], preferred_element_type=F32)
    for h in range(N_IDX_HEADS):
        qi_ref[0, h] = qi[:, h * LANES:(h + 1) * LANES].astype(BF16)
    q = jnp.dot(cq, wuq_ref[...], preferred_element_type=F32).astype(BF16)
    for h in range(N_ATT_HEADS):
        ql = jnp.dot(q[:, h * LANES:(h + 1) * LANES], wuk_ref[h], preferred_element_type=F32)
        ql_ref[0, h] = (ql * (ATT_SCALE * LOG2_E)).astype(ATT_MM_DTYPE)


def _dsa_prep(proj, small, p, b, s):
    tm = min(512, s)
    nq = s // tm
    row = lambda bi, i: bi * nq + i
    full = lambda shape: pl.BlockSpec(shape, lambda bi, i: tuple(0 for _ in shape))
    return pl.pallas_call(
        _dsa_prep_kernel,
        grid=(b, nq),
        in_specs=[pl.BlockSpec((tm, KV_LORA), lambda bi, i: (row(bi, i), COL_CKV // KV_LORA)),
                  pl.BlockSpec((tm, Q_LORA), lambda bi, i: (row(bi, i), COL_CQ // Q_LORA)),
                  pl.BlockSpec((tm, LANES), lambda bi, i: (row(bi, i), 0)),
                  full((1, Q_LORA)), full((1, KV_LORA)), full((1, LANES)),
                  full((Q_LORA, N_IDX_HEADS * LANES)), full((Q_LORA, N_ATT_HEADS * LANES)),
                  full((N_ATT_HEADS, LANES, KV_LORA))],
        out_specs=[pl.BlockSpec((1, N_IDX_HEADS, tm, LANES), lambda bi, i: (bi, 0, i, 0)),
                   pl.BlockSpec((1, N_ATT_HEADS, tm, KV_LORA), lambda bi, i: (bi, 0, i, 0)),
                   pl.BlockSpec((tm, KV_LORA), lambda bi, i: (row(bi, i), 0)),
                   pl.BlockSpec((tm, LANES), lambda bi, i: (row(bi, i), 0))],
        out_shape=[jax.ShapeDtypeStruct((b, N_IDX_HEADS, s, LANES), BF16),
                   jax.ShapeDtypeStruct((b, N_ATT_HEADS, s, KV_LORA), ATT_MM_DTYPE),
                   jax.ShapeDtypeStruct((b * s, KV_LORA), ATT_MM_DTYPE),
                   jax.ShapeDtypeStruct((b * s, LANES), BF16)],
        compiler_params=_cparams(("parallel", "parallel")),
        name="dsa_prep",
    )(proj, proj, small, p["q_norm_g"], p["kv_norm_g"], p["k_norm_g"], p["w_qidx"], p["w_uq"], p["w_uk"])


def _dsa_kernel(qi_ref, ql_ref, small_ref, kidx_ref, ckv_ref, wuv_ref, o_ref,
                key_ref, whb_ref, d_ref, s_ref, sb_ref, p_ref, bias_ref,
                m_ref, l_ref, rmax_ref, acc_ref, cand_ref, *, topk, seq_bits):
    qb = pl.program_id(1)
    QB = Q_BLOCK
    TK = key_ref.shape[2]
    NT = TK // LANES
    nkc = (qb * QB + QB + TK - 1) // TK
    q_pos = qb * QB + lax.broadcasted_iota(I32, (QB, 1), 0)
    lane_tk = lax.broadcasted_iota(I32, (QB, TK), 1)

    sm = small_ref[...]
    for h in range(N_IDX_HEADS):
        whb_ref[h] = jnp.broadcast_to(sm[:, SM_WIDX + h:SM_WIDX + h + 1] * IDX_SCALE, (QB, LANES))
    qi = qi_ref[0].reshape(N_IDX_HEADS * QB, LANES)

    def score_chunks(kcs):
        offs = [pl.multiple_of(kc * TK, TK) for kc in kcs]
        for j in range(len(kcs)):
            kx = kidx_ref[pl.ds(offs[j], TK), :]
            d_ref[j] = lax.dot_general(qi, kx, _NT, preferred_element_type=F32)
        for j, kc in enumerate(kcs):
            for t in range(NT):
                sc = jnp.zeros((QB, LANES), F32)
                for h in range(N_IDX_HEADS):
                    d = d_ref[j, h * QB:(h + 1) * QB, t * LANES:(t + 1) * LANES]
                    sc = sc + jnp.maximum(d, 0.0) * whb_ref[h]
                bits = pltpu.bitcast(sc, I32)
                key = bits ^ ((bits >> 31) & 0x7FFFFFFF)
                kpos = offs[j] + t * LANES + lax.broadcasted_iota(I32, (QB, LANES), 1)
                key_ref[kc, :, t * LANES:(t + 1) * LANES] = jnp.where(kpos <= q_pos, key, KEY_MIN)

    def score_pair(i, carry):
        score_chunks([2 * i, 2 * i + 1])
        return carry

    lax.fori_loop(0, nkc // 2, score_pair, 0)

    @pl.when(nkc % 2 == 1)
    def _():
        score_chunks([nkc - 1])

    kk = jnp.minimum(topk, q_pos + 1).astype(F32)

    def count(pred):
        def body(kc, acc):
            k = key_ref[kc]
            for t in range(NT):
                acc = acc + jnp.where(pred(k[:, t * LANES:(t + 1) * LANES], kc * TK + t * LANES), 1.0, 0.0)
            return acc
        acc = lax.fori_loop(0, nkc, body, jnp.zeros((QB, LANES), F32))
        return jnp.sum(acc, axis=-1, keepdims=True)

    def value_bit(i, u):
        cand_u = u | jnp.left_shift(jnp.int32(1), 31 - i)
        cand = cand_u ^ KEY_MIN
        cnt = count(lambda k, base: k >= cand)
        return jnp.where(cnt >= kk, cand_u, u)

    def full_select():
        t = lax.fori_loop(0, 32, value_bit, jnp.zeros((QB, 1), I32)) ^ KEY_MIN
        return t, count(lambda k, base: k > t), count(lambda k, base: k >= t)

    RG = CAND_ROWS

    def collect(rg, carry):
        r0 = pl.multiple_of(rg * RG, RG)

        def insert(kc, tops):
            tops = list(tops)
            for t in range(NT):
                x = key_ref[kc, pl.ds(r0, RG), t * LANES:(t + 1) * LANES]
                for i in range(TOP_PER_LANE):
                    keep = tops[i] >= x
                    tops[i], x = jnp.where(keep, tops[i], x), jnp.where(keep, x, tops[i])
            return tuple(tops)

        tops = lax.fori_loop(0, nkc, insert, tuple(jnp.full((RG, LANES), KEY_MIN, I32) for _ in range(TOP_PER_LANE)))
        for i in range(TOP_PER_LANE):
            cand_ref[i, pl.ds(r0, RG), :] = tops[i]
        return carry

    lax.fori_loop(0, QB // RG, collect, 0)

    def count_cand(pred):
        acc = jnp.zeros((QB, LANES), F32)
        for i in range(TOP_PER_LANE):
            acc = acc + jnp.where(pred(cand_ref[i]), 1.0, 0.0)
        return jnp.sum(acc, axis=-1, keepdims=True)

    def cand_bit(i, u):
        cand_u = u | jnp.left_shift(jnp.int32(1), 31 - i)
        cand = cand_u ^ KEY_MIN
        return jnp.where(count_cand(lambda k: k >= cand) >= kk, cand_u, u)

    thr_c = lax.fori_loop(0, 32, cand_bit, jnp.zeros((QB, 1), I32)) ^ KEY_MIN
    lane_full = jnp.max(jnp.where(cand_ref[TOP_PER_LANE - 1] >= thr_c, 1.0, 0.0)) > 0.0
    thr, c_gt, c_ge = lax.cond(
        lane_full, full_select,
        lambda: (thr_c, count_cand(lambda k: k > thr_c), count_cand(lambda k: k >= thr_c)))
    need = kk - c_gt
    lane128 = lax.broadcasted_iota(I32, (QB, LANES), 1)

    def tie_cut():
        def index_bit(i, pcut):
            cand = pcut | jnp.left_shift(jnp.int32(1), seq_bits - 1 - i)
            cnt = count(lambda k, base: (k == thr) & (base + lane128 < cand))
            return jnp.where(cnt < need, cand, pcut)
        return lax.fori_loop(0, seq_bits, index_bit, jnp.zeros((QB, 1), I32))

    has_ties = jnp.max(c_ge - kk) > 0.0
    pcut = lax.cond(has_ties, tie_cut, lambda: jnp.full((QB, 1), 2 ** seq_bits - 1, I32))

    m_ref[...] = jnp.full(m_ref.shape, NEG_BIG, F32)
    l_ref[...] = jnp.zeros(l_ref.shape, F32)
    acc_ref[...] = jnp.zeros(acc_ref.shape, F32)
    HQ = ATT_QUARTER_HEADS
    n_quarters = N_ATT_HEADS // HQ
    RB = ATT_ROW_BLOCK
    n_rb = QB // RB

    def attend(kcs):
        offs = [pl.multiple_of(kc * TK, TK) for kc in kcs]
        for j, kc in enumerate(kcs):
            k = key_ref[kc]
            sel = (k > thr) | ((k == thr) & (offs[j] + lane_tk <= pcut))
            bias_ref[j] = jnp.where(sel, 0.0, NEG_BIG).astype(BF16)

        def logits(j, qt):
            ql = ql_ref[0, qt * HQ:(qt + 1) * HQ].reshape(HQ * QB, KV_LORA)
            kv = ckv_ref[pl.ds(offs[j], TK), :]
            s_ref[qt] = lax.dot_general(ql, kv, _NT, preferred_element_type=F32)

        def row_max(j, qt, rb):
            bias = bias_ref[j, rb * RB:(rb + 1) * RB, :]
            for hh in range(HQ):
                r0 = hh * QB + rb * RB
                s = s_ref[qt, r0:r0 + RB, :].astype(BF16) + bias
                sb_ref[qt, r0:r0 + RB, :] = s
                mx = s[:, 0:LANES]
                for t in range(1, NT):
                    mx = jnp.maximum(mx, s[:, t * LANES:(t + 1) * LANES])
                g0 = qt * HQ * QB + r0
                rmax_ref[g0:g0 + RB, :] = jnp.broadcast_to(
                    jnp.max(mx.astype(F32), axis=-1, keepdims=True), (RB, LANES))

        def row_probs(qt, rb):
            for hh in range(HQ):
                r0 = hh * QB + rb * RB
                g0 = qt * HQ * QB + r0
                m_prev = m_ref[g0:g0 + RB, :]
                m_new = jnp.maximum(m_prev, rmax_ref[g0:g0 + RB, :])
                alpha = jnp.exp2(m_prev - m_new)
                m_b = m_new.astype(BF16)
                psum = None
                for t in range(NT):
                    cols = slice(t * LANES, (t + 1) * LANES)
                    p = jnp.exp2(sb_ref[qt, r0:r0 + RB, cols] - m_b)
                    psum = p if psum is None else psum + p
                    p_ref[qt, r0:r0 + RB, cols] = p.astype(ATT_MM_DTYPE)
                l_ref[g0:g0 + RB, :] = alpha * l_ref[g0:g0 + RB, :] + psum.astype(F32)
                m_ref[g0:g0 + RB, :] = m_new
                for c in range(KV_LORA // LANES):
                    cols = slice(c * LANES, (c + 1) * LANES)
                    acc_ref[g0:g0 + RB, cols] = acc_ref[g0:g0 + RB, cols] * alpha

        items = [(j, qt) for j in range(len(kcs)) for qt in range(n_quarters)]
        logits(*items[0])
        for i, (j, qt) in enumerate(items):
            if i + 1 < len(items):
                logits(*items[i + 1])
            for rb in range(n_rb):
                row_max(j, qt, rb)
            for rb in range(n_rb):
                row_probs(qt, rb)
            rows = slice(qt * HQ * QB, (qt + 1) * HQ * QB)
            kv = ckv_ref[pl.ds(offs[j], TK), :]
            acc_ref[rows, :] += jnp.dot(p_ref[qt], kv, preferred_element_type=F32)

    G = ATT_CHUNKS_PER_STEP

    def attend_group(i, carry):
        attend([G * i + j for j in range(G)])
        return carry

    lax.fori_loop(0, nkc // G, attend_group, 0)
    done = (nkc // G) * G
    piece = G // 2
    while piece >= 1:
        take = ((nkc - done) // piece) % 2 == 1 if piece > 1 else (nkc - done) % 2 == 1
        start = done + ((nkc - done) // (2 * piece)) * (2 * piece)

        @pl.when(take)
        def _(start=start, piece=piece):
            attend([start + j for j in range(piece)])

        piece //= 2

    for pr in range(N_ATT_HEADS // 2):
        out = jnp.zeros((QB, LANES), F32)
        for half in range(2):
            h = 2 * pr + half
            rows = slice(h * QB, (h + 1) * QB)
            l_row = jnp.sum(l_ref[rows, :], axis=-1, keepdims=True)
            o_lat = (acc_ref[rows, :] / l_row).astype(BF16)
            out = out + jnp.dot(o_lat, wuv_ref[h], preferred_element_type=F32)
        o_ref[:, pr * LANES:(pr + 1) * LANES] = out.astype(BF16)


def _dsa(qi, ql, small, kidx, ckvn, wuv, b, s):
    QB = Q_BLOCK
    TK = min(DSA_TK, s)
    nq = s // QB
    topk = min(TOPK_MAX, s // 4)
    seq_bits = (s - 1).bit_length()
    resident = dict(pipeline_mode=pl.Buffered(1))
    return pl.pallas_call(
        functools.partial(_dsa_kernel, topk=topk, seq_bits=seq_bits),
        grid=(b, nq),
        in_specs=[pl.BlockSpec((1, N_IDX_HEADS, QB, LANES), lambda bi, i: (bi, 0, i, 0)),
                  pl.BlockSpec((1, N_ATT_HEADS, QB, KV_LORA), lambda bi, i: (bi, 0, i, 0)),
                  pl.BlockSpec((QB, LANES), lambda bi, i: (bi * nq + i, 0)),
                  pl.BlockSpec((s, LANES), lambda bi, i: (bi, 0), **resident),
                  pl.BlockSpec((s, KV_LORA), lambda bi, i: (bi, 0), **resident),
                  pl.BlockSpec((N_ATT_HEADS, KV_LORA, LANES), lambda bi, i: (0, 0, 0))],
        out_specs=pl.BlockSpec((QB, N_ATT_HEADS * ATT_V_DIM), lambda bi, i: (bi * nq + i, 0)),
        out_shape=jax.ShapeDtypeStruct((b * s, N_ATT_HEADS * ATT_V_DIM), BF16),
        scratch_shapes=[pltpu.VMEM((s // TK, QB, TK), I32),
                        pltpu.VMEM((N_IDX_HEADS, QB, LANES), F32),
                        pltpu.VMEM((2, N_IDX_HEADS * QB, TK), F32),
                        pltpu.VMEM((N_ATT_HEADS // ATT_QUARTER_HEADS, ATT_QUARTER_HEADS * QB, TK), F32),
                        pltpu.VMEM((N_ATT_HEADS // ATT_QUARTER_HEADS, ATT_QUARTER_HEADS * QB, TK), BF16),
                        pltpu.VMEM((N_ATT_HEADS // ATT_QUARTER_HEADS, ATT_QUARTER_HEADS * QB, TK), ATT_MM_DTYPE),
                        pltpu.VMEM((ATT_CHUNKS_PER_STEP, QB, TK), BF16),
                        pltpu.VMEM((N_ATT_HEADS * QB, LANES), F32),
                        pltpu.VMEM((N_ATT_HEADS * QB, LANES), F32),
                        pltpu.VMEM((N_ATT_HEADS * QB, LANES), F32),
                        pltpu.VMEM((N_ATT_HEADS * QB, KV_LORA), F32),
                        pltpu.VMEM((TOP_PER_LANE, QB, LANES), I32)],
        compiler_params=_cparams(("parallel", "arbitrary")),
        name="dsa",
    )(qi, ql, small, kidx, ckvn, wuv)


def _merge_kernel(x_ref, ya_ref, yb_ref, yc_ref, ga_ref, gb_ref, gc_ref, wa_ref, wb_ref, wc_ref, wm_ref, o_ref):
    def branch(y_ref, w_ref, g_ref):
        y = jnp.dot(y_ref[...], w_ref[...], preferred_element_type=F32)
        return jax.nn.sigmoid(g_ref[...].astype(F32)) * y

    merged = branch(ya_ref, wa_ref, ga_ref) + branch(yb_ref, wb_ref, gb_ref) + branch(yc_ref, wc_ref, gc_ref)
    o_ref[...] = x_ref[...] + jnp.dot(merged.astype(BF16), wm_ref[...], preferred_element_type=F32)


def _merge(x2, ya, yb, yc, proj, p):
    t, d = x2.shape
    tm = min(512, t)
    tok = lambda cols, cb: pl.BlockSpec((tm, cols), lambda i: (i, cb))
    full = lambda shape: pl.BlockSpec(shape, lambda i: (0, 0))
    return pl.pallas_call(
        _merge_kernel,
        grid=(t // tm,),
        in_specs=[tok(d, 0), tok(D_INNER, 0), tok(d, 0), tok(d, 0),
                  tok(d, COL_GA // d), tok(d, COL_GB // d), tok(d, COL_GC // d),
                  full((D_INNER, d)), full((d, d)), full((d, d)), full((d, d))],
        out_specs=tok(d, 0),
        out_shape=jax.ShapeDtypeStruct((t, d), F32),
        compiler_params=_cparams(("parallel",)),
        name="merge",
    )(x2, ya, yb, yc, proj, proj, proj, p["ssm_w_out"], p["sc_w_out"], p["att_w_out"], p["mix_w_out"])


def _first_argmax(v, lane):
    m = jnp.max(v, axis=-1, keepdims=True)
    idx = jnp.min(jnp.where(v == m, lane, LANES), axis=-1, keepdims=True)
    return m, idx


def _moe_kernel(x_ref, g_ref, wr_ref, br_ref, wg_ref, wu_ref, wd_ref, fg_ref, o_ref, h_ref, comb_ref, acc_ref,
                *, final_norm):
    e = pl.program_id(1)

    @pl.when(e == 0)
    def _():
        x = x_ref[...]
        h = x * lax.rsqrt(jnp.mean(x * x, axis=-1, keepdims=True) + EPS) * g_ref[...]
        h_ref[...] = h.astype(BF16)
        logits = jnp.dot(h, wr_ref[...], precision=lax.Precision.HIGHEST, preferred_element_type=F32) + br_ref[...]
        lane = lax.broadcasted_iota(I32, logits.shape, 1)
        neg_inf = -jnp.inf
        lg = jnp.where(lane < N_EXPERT_GROUPS, logits, neg_inf)
        eg = jnp.exp(lg - jnp.max(lg, axis=-1, keepdims=True))
        pg = eg / jnp.sum(eg, axis=-1, keepdims=True)
        p_group, g_sel = _first_argmax(jnp.where(lane < N_EXPERT_GROUPS, pg, neg_inf), lane)
        lo = N_EXPERT_GROUPS + g_sel * EXPERTS_PER_GROUP
        in_group = (lane >= lo) & (lane < lo + EXPERTS_PER_GROUP)
        le = jnp.where(in_group, logits, neg_inf)
        ee = jnp.exp(le - jnp.max(le, axis=-1, keepdims=True))
        pe = jnp.where(in_group, ee / jnp.sum(ee, axis=-1, keepdims=True), neg_inf)
        p1, i1 = _first_argmax(pe, lane)
        p2, i2 = _first_argmax(jnp.where(lane == i1, neg_inf, pe), lane)
        tot = p1 + p2
        comb_ref[...] = (jnp.where(lane == i1, p_group * (p1 / tot), 0.0)
                         + jnp.where(lane == i2, p_group * (p2 / tot), 0.0))
        acc_ref[...] = jnp.zeros(acc_ref.shape, F32)

    h = h_ref[...]
    lane = lax.broadcasted_iota(I32, comb_ref.shape, 1)
    c_e = jnp.sum(jnp.where(lane == e + N_EXPERT_GROUPS, comb_ref[...], 0.0), axis=-1, keepdims=True)
    gate = jnp.dot(h, wg_ref[0], preferred_element_type=F32)
    up = jnp.dot(h, wu_ref[0], preferred_element_type=F32)
    hid = (_silu(gate) * up * c_e).astype(BF16)
    acc_ref[...] += jnp.dot(hid, wd_ref[0], preferred_element_type=F32)

    @pl.when(e == N_EXPERTS - 1)
    def _():
        y = x_ref[...] + acc_ref[...]
        if final_norm:
            y = y * lax.rsqrt(jnp.mean(y * y, axis=-1, keepdims=True) + EPS) * fg_ref[...]
        o_ref[...] = y


def _moe(x2, p, final_g, final_norm):
    t, d = x2.shape
    tm = min(1024, t)
    full = lambda shape: pl.BlockSpec(shape, lambda i, e: (0, 0))
    return pl.pallas_call(
        functools.partial(_moe_kernel, final_norm=final_norm),
        grid=(t // tm, N_EXPERTS),
        in_specs=[pl.BlockSpec((tm, d), lambda i, e: (i, 0)), full((1, d)), full((d, LANES)), full((1, LANES)),
                  pl.BlockSpec((1, d, D_EXPERT), lambda i, e: (e, 0, 0)),
                  pl.BlockSpec((1, d, D_EXPERT), lambda i, e: (e, 0, 0)),
                  pl.BlockSpec((1, D_EXPERT, d), lambda i, e: (e, 0, 0)),
                  full((1, d))],
        out_specs=pl.BlockSpec((tm, d), lambda i, e: (i, 0)),
        out_shape=jax.ShapeDtypeStruct((t, d), F32),
        scratch_shapes=[pltpu.VMEM((tm, d), BF16), pltpu.VMEM((tm, LANES), F32), pltpu.VMEM((tm, d), F32)],
        compiler_params=_cparams(("parallel", "arbitrary")),
        name="moe",
    )(x2, p["norm2_g"], p["w_router"], p["b_router"], p["w_gate"], p["w_up"], p["w_down"], final_g)


def _pad_cols(w, width, offset, total):
    return jnp.pad(w, [(0, 0)] * (w.ndim - 1) + [(offset, total - offset - width)])


def _layer_params(l, norm1_g, w_in, ssm_conv_w, ssm_conv_b, ssm_dt_bias, ssm_a_log, ssm_d, ssm_norm_g, ssm_w_out,
                  sc_conv_w, sc_w_out, att_q_norm_g, att_kv_norm_g, idx_k_norm_g, att_w_uq, idx_w_q, att_w_uk,
                  att_w_uv, att_w_out, mix_w_out, norm2_g, moe_w_group, moe_b_group, moe_w_expert, moe_b_expert,
                  moe_w_gate, moe_w_up, moe_w_down):
    w = w_in[l]
    o = 0
    seg = {}
    for name, size in (("z", 2048), ("xbc", 3072), ("dt", 32), ("scb", 1024), ("scc", 1024), ("scx", 1024),
                       ("cq", Q_LORA), ("ckv", KV_LORA), ("kidx", D_IDX), ("widx", N_IDX_HEADS),
                       ("ga", 1024), ("gb", 1024), ("gc", 1024)):
        seg[name] = w[:, o:o + size]
        o += size
    d = w.shape[0]
    small = jnp.concatenate([seg["dt"], seg["widx"], jnp.zeros((d, SM_KIDX - SM_WIDX - N_IDX_HEADS), F32),
                             seg["kidx"]], axis=1)
    w_perm = jnp.concatenate([seg["z"], seg["xbc"], seg["scb"], seg["scc"], seg["scx"], seg["ga"], seg["gb"],
                              seg["gc"], seg["ckv"], seg["cq"], small,
                              jnp.zeros((d, N_PROJ - COL_SMALL - LANES), F32)], axis=1).astype(BF16)
    head_of_lane = jnp.arange(LANES)[:, None]
    head_of_chan = (jnp.arange(D_INNER) // SSM_HEAD_DIM)[None, :]
    wqi = idx_w_q[l]
    wuq = att_w_uq[l]
    wuv = att_w_uv[l]
    wuv_pad = jnp.stack([_pad_cols(wuv[h], ATT_V_DIM, (h % 2) * ATT_V_DIM, LANES) for h in range(N_ATT_HEADS)])
    return {
        "norm1_g": norm1_g[l][None, :], "w_perm": w_perm,
        "conv_w": ssm_conv_w[l], "conv_b": ssm_conv_b[l][None, :],
        "dt_b": _pad_cols(ssm_dt_bias[l][None, :], N_SSM_HEADS, SM_DT, LANES), "dt_bt": ssm_dt_bias[l][:, None],
        "a_log": _pad_cols(ssm_a_log[l][None, :], N_SSM_HEADS, SM_DT, LANES), "a_logt": ssm_a_log[l][:, None],
        "d_skip": jnp.repeat(ssm_d[l], SSM_HEAD_DIM)[None, :], "norm_g": ssm_norm_g[l][None, :],
        "expand": (head_of_lane == head_of_chan).astype(BF16),
        "sc_conv_w": sc_conv_w[l],
        "q_norm_g": att_q_norm_g[l][None, :], "kv_norm_g": att_kv_norm_g[l][None, :],
        "k_norm_g": _pad_cols(idx_k_norm_g[l][None, :], D_IDX, SM_KIDX, LANES),
        "w_qidx": _pad_cols(wqi, D_IDX, SM_KIDX, LANES).reshape(Q_LORA, N_IDX_HEADS * LANES).astype(BF16),
        "w_uq": _pad_cols(wuq, ATT_HEAD_DIM, 0, LANES).reshape(Q_LORA, N_ATT_HEADS * LANES).astype(BF16),
        "w_uk": jnp.pad(att_w_uk[l], ((0, 0), (0, LANES - ATT_HEAD_DIM), (0, 0))).astype(BF16),
        "w_uv": wuv_pad.astype(BF16),
        "ssm_w_out": ssm_w_out[l].astype(BF16), "sc_w_out": sc_w_out[l].astype(BF16),
        "att_w_out": att_w_out[l].astype(BF16), "mix_w_out": mix_w_out[l].astype(BF16),
        "norm2_g": norm2_g[l][None, :],
        "w_router": _pad_cols(jnp.concatenate([moe_w_group[l], moe_w_expert[l]], axis=1),
                              N_EXPERT_GROUPS + N_EXPERTS, 0, LANES),
        "b_router": _pad_cols(jnp.concatenate([moe_b_group[l], moe_b_expert[l]])[None, :],
                              N_EXPERT_GROUPS + N_EXPERTS, 0, LANES),
        "w_gate": moe_w_gate[l].astype(BF16), "w_up": moe_w_up[l].astype(BF16), "w_down": moe_w_down[l].astype(BF16),
    }


def kernel(x, norm1_g, w_in, ssm_conv_w, ssm_conv_b, ssm_dt_bias, ssm_a_log, ssm_d, ssm_norm_g, ssm_w_out, sc_conv_w, sc_w_out, att_q_norm_g, att_kv_norm_g, idx_k_norm_g, att_w_uq, idx_w_q, att_w_uk, att_w_uv, att_w_out, mix_w_out, norm2_g, moe_w_group, moe_b_group, moe_w_expert, moe_b_expert, moe_w_gate, moe_w_up, moe_w_down, final_norm_g):
    b, s, d = x.shape
    depth = w_in.shape[0]
    assert d == D_MODEL and s % DSA_TK == 0 or s < DSA_TK
    x2 = x.reshape(b * s, d)
    final_g = final_norm_g[None, :]
    for l in range(depth):
        p = _layer_params(l, norm1_g, w_in, ssm_conv_w, ssm_conv_b, ssm_dt_bias, ssm_a_log, ssm_d, ssm_norm_g,
                          ssm_w_out, sc_conv_w, sc_w_out, att_q_norm_g, att_kv_norm_g, idx_k_norm_g, att_w_uq,
                          idx_w_q, att_w_uk, att_w_uv, att_w_out, mix_w_out, norm2_g, moe_w_group, moe_b_group,
                          moe_w_expert, moe_b_expert, moe_w_gate, moe_w_up, moe_w_down)
        proj, small = _in_proj(x2, p["norm1_g"], p["w_perm"])
        dtt = jnp.swapaxes(small[:, SM_DT:SM_DT + N_SSM_HEADS].reshape(b, s, N_SSM_HEADS), 1, 2)
        ya = _ssd(proj, small, dtt, p, b, s)
        yb = _sconv(proj, p["sc_conv_w"], b, s)
        qi, ql, ckvn, kidx = _dsa_prep(proj, small, p, b, s)
        yc = _dsa(qi, ql, small, kidx, ckvn, p["w_uv"], b, s)
        x2 = _merge(x2, ya, yb, yc, proj, p)
        x2 = _moe(x2, p, final_g, final_norm=(l == depth - 1))
    return x2.reshape(b, s, d)
```

```python
import functools

import jax
import jax.numpy as jnp
from jax import lax
from jax.experimental import pallas as pl
from jax.experimental.pallas import tpu as pltpu

F32 = jnp.float32
BF16 = jnp.bfloat16
I32 = jnp.int32
ATT_MM_DTYPE = jnp.bfloat16

EPS = 1e-6
LANES = 128
N_SSM_HEADS = 32
SSM_HEAD_DIM = 64
N_SSM_GROUPS = 4
D_STATE = 128
SSM_CONV = 4
SSM_CHUNK = 128
SC_CONV = 3
N_ATT_HEADS = 16
ATT_HEAD_DIM = 64
ATT_V_DIM = 64
Q_LORA = 384
KV_LORA = 256
N_IDX_HEADS = 8
D_IDX = 64
TOPK_MAX = 256
Q_BLOCK = 128
ATT_SCALE = ATT_HEAD_DIM ** -0.5
LOG2_E = 1.4426950408889634
IDX_SCALE = (N_IDX_HEADS * D_IDX) ** -0.5
N_EXPERT_GROUPS = 4
EXPERTS_PER_GROUP = 4
N_EXPERTS = 16
D_EXPERT = 512

D_MODEL = 1024
D_INNER = 2048
COL_Z = 0
COL_XS = 2048
COL_BM = 4096
COL_CM = 4608
COL_SCB = 5120
COL_SCC = 6144
COL_SCX = 7168
COL_GA = 8192
COL_GB = 9216
COL_GC = 10240
COL_CKV = 11264
COL_CQ = 11520
COL_SMALL = 11904
N_PROJ = 12288
SM_DT = 0
SM_WIDX = 32
SM_KIDX = 64

PROJ_TM = 1024
PROJ_TN = 2048
DSA_TK = 512
ATT_QUARTER_HEADS = 4
ATT_ROW_BLOCK = 16
ATT_CHUNKS_PER_STEP = 4
SCORE_CHUNKS_PER_STEP = 4
TOP_PER_LANE = 12
CAND_ROWS = 32
KEY_MIN = -2 ** 31
NEG_BIG = -2.0 ** 100
VMEM_LIMIT = 60 * 1024 * 1024

_NT = (((1,), (1,)), ((), ()))


def _cparams(sem):
    return pltpu.CompilerParams(dimension_semantics=sem, vmem_limit_bytes=VMEM_LIMIT)


def _silu(v):
    half = 0.5 * v
    return half + half * jnp.tanh(half)


def _softplus(v):
    return jnp.maximum(v, 0.0) + jnp.log(1.0 + jnp.exp(-jnp.abs(v)))


def _in_proj_kernel(x_ref, g_ref, w_ref, o_ref, small_ref, hn_ref, *, small_j, small_off):
    j = pl.program_id(1)

    @pl.when(j == 0)
    def _():
        x = x_ref[...]
        hn = x * lax.rsqrt(jnp.mean(x * x, axis=-1, keepdims=True) + EPS) * g_ref[...]
        hn_ref[...] = hn.astype(BF16)

    acc = jnp.dot(hn_ref[...], w_ref[...], preferred_element_type=F32)
    o_ref[...] = acc.astype(BF16)

    @pl.when(j == small_j)
    def _():
        small_ref[...] = acc[:, small_off:small_off + LANES]


def _in_proj(x2, g, w_perm):
    t, d = x2.shape
    tm = min(PROJ_TM, t)
    small_j, small_off = COL_SMALL // PROJ_TN, COL_SMALL % PROJ_TN
    return pl.pallas_call(
        functools.partial(_in_proj_kernel, small_j=small_j, small_off=small_off),
        grid=(t // tm, N_PROJ // PROJ_TN),
        in_specs=[pl.BlockSpec((tm, d), lambda i, j: (i, 0)),
                  pl.BlockSpec((1, d), lambda i, j: (0, 0)),
                  pl.BlockSpec((d, PROJ_TN), lambda i, j: (0, j))],
        out_specs=[pl.BlockSpec((tm, PROJ_TN), lambda i, j: (i, j)),
                   pl.BlockSpec((tm, LANES), lambda i, j: (i, 0))],
        out_shape=[jax.ShapeDtypeStruct((t, N_PROJ), BF16), jax.ShapeDtypeStruct((t, LANES), F32)],
        scratch_shapes=[pltpu.VMEM((tm, d), BF16)],
        compiler_params=_cparams(("parallel", "arbitrary")),
        name="in_proj",
    )(x2, g, w_perm)


def _split_dot(a, e_bf16):
    hi = a.astype(BF16)
    lo = (a - hi.astype(F32)).astype(BF16)
    return (jnp.dot(hi, e_bf16, preferred_element_type=F32) + jnp.dot(lo, e_bf16, preferred_element_type=F32))


def _ssd_kernel(z_ref, xs_ref, bm_ref, cm_ref, small_ref, dtt_ref, convw_ref, convb_ref, dtb_ref, dtbt_ref,
                alog_ref, alogt_ref, dskip_ref, ng_ref, e_ref, o_ref, buf_ref, xbc_ref, state_ref, y_ref):
    c = pl.program_id(1)
    L = SSM_CHUNK
    GW = D_INNER // N_SSM_GROUPS
    HALO = 8

    @pl.when(c == 0)
    def _():
        buf_ref[0:HALO, :] = jnp.zeros((HALO, buf_ref.shape[1]), F32)
        state_ref[...] = jnp.zeros(state_ref.shape, F32)

    buf_ref[HALO:HALO + L, 0:D_INNER] = xs_ref[...].astype(F32)
    buf_ref[HALO:HALO + L, D_INNER:D_INNER + GW] = bm_ref[...].astype(F32)
    buf_ref[HALO:HALO + L, D_INNER + GW:D_INNER + 2 * GW] = cm_ref[...].astype(F32)
    for cc in range(buf_ref.shape[1] // GW):
        cols = slice(cc * GW, (cc + 1) * GW)
        acc = jnp.broadcast_to(convb_ref[:, cols], (L, GW))
        for j in range(SSM_CONV):
            r0 = HALO - (SSM_CONV - 1) + j
            acc = acc + convw_ref[j:j + 1, cols] * buf_ref[r0:r0 + L, cols]
        xbc_ref[:, cols] = _silu(acc)
    buf_ref[0:HALO, :] = buf_ref[L:L + HALO, :]

    row = lax.broadcasted_iota(I32, (L, L), 0)
    col = lax.broadcasted_iota(I32, (L, L), 1)
    causal = row >= col
    tril = causal.astype(F32)
    triu = (row <= col).astype(F32)
    dt = _softplus(small_ref[...] + dtb_ref[...])
    da = dt * (-jnp.exp(alog_ref[...]))
    cs = jnp.dot(tril, da, precision=lax.Precision.HIGHEST, preferred_element_type=F32)
    dtt = _softplus(dtt_ref[0] + dtbt_ref[...])
    dat = dtt * (-jnp.exp(alogt_ref[...]))
    cst = jnp.dot(dat, triu, precision=lax.Precision.HIGHEST, preferred_element_type=F32)
    cs_last = cs[L - 1:L, :]
    e = e_ref[...]
    dt_e = _split_dot(dt, e)
    dte_e = _split_dot(jnp.exp(cs_last - cs), e)
    eo_e = _split_dot(jnp.exp(cs), e)
    cd_e = _split_dot(jnp.broadcast_to(jnp.exp(cs_last), (8, LANES)), e)[0:1, :]

    lane = lax.broadcasted_iota(I32, (L, LANES), 1)
    lo_half = lane < SSM_HEAD_DIM
    for g in range(N_SSM_GROUPS):
        gc = slice(g * GW, (g + 1) * GW)
        bm_g = xbc_ref[:, D_INNER + g * D_STATE:D_INNER + (g + 1) * D_STATE]
        cm_g = xbc_ref[:, D_INNER + GW + g * D_STATE:D_INNER + GW + (g + 1) * D_STATE]
        cm_b = cm_g.astype(BF16)
        cb = lax.dot_general(cm_b, bm_g.astype(BF16), _NT, preferred_element_type=F32)
        xs_g = xbc_ref[:, gc]
        xdt_g = xs_g * dt_e[:, gc]
        st = state_ref[g]
        y_off = jnp.dot(cm_b, st.astype(BF16), preferred_element_type=F32) * eo_e[:, gc]
        for pr in range(GW // LANES):
            x_pair = xdt_g[:, pr * LANES:(pr + 1) * LANES]
            y_pair = y_off[:, pr * LANES:(pr + 1) * LANES]
            for half in range(2):
                h = g * (GW // SSM_HEAD_DIM) + pr * 2 + half
                seg = cs[:, h:h + 1] - cst[h:h + 1, :]
                dec = jnp.exp(jnp.where(causal, seg, -jnp.inf))
                m = (cb * dec).astype(BF16)
                keep = lo_half if half == 0 else jnp.logical_not(lo_half)
                x_h = jnp.where(keep, x_pair, 0.0).astype(BF16)
                y_pair = y_pair + jnp.dot(m, x_h, preferred_element_type=F32)
            y_ref[:, g * GW + pr * LANES:g * GW + (pr + 1) * LANES] = y_pair
        xd = (xdt_g * dte_e[:, gc]).astype(BF16)
        new = jnp.dot(bm_g.T.astype(BF16), xd, preferred_element_type=F32)
        state_ref[g] = cd_e[:, gc] * st + new
        yg = (y_ref[:, gc] + dskip_ref[:, gc] * xs_g) * _silu(z_ref[:, gc].astype(F32))
        yg = yg * lax.rsqrt(jnp.mean(yg * yg, axis=-1, keepdims=True) + EPS)
        o_ref[:, gc] = (yg * ng_ref[:, gc]).astype(BF16)


def _ssd(proj, small, dtt, p, b, s):
    L = SSM_CHUNK
    nc = s // L
    conv_dim = D_INNER + 2 * N_SSM_GROUPS * D_STATE
    row = lambda bi, ci: bi * nc + ci
    full = lambda shape: pl.BlockSpec(shape, lambda bi, ci: tuple(0 for _ in shape))
    return pl.pallas_call(
        _ssd_kernel,
        grid=(b, nc),
        in_specs=[pl.BlockSpec((L, D_INNER), lambda bi, ci: (row(bi, ci), COL_Z // D_INNER)),
                  pl.BlockSpec((L, D_INNER), lambda bi, ci: (row(bi, ci), COL_XS // D_INNER)),
                  pl.BlockSpec((L, 512), lambda bi, ci: (row(bi, ci), COL_BM // 512)),
                  pl.BlockSpec((L, 512), lambda bi, ci: (row(bi, ci), COL_CM // 512)),
                  pl.BlockSpec((L, LANES), lambda bi, ci: (row(bi, ci), 0)),
                  pl.BlockSpec((1, N_SSM_HEADS, L), lambda bi, ci: (bi, 0, ci)),
                  full((SSM_CONV, conv_dim)), full((1, conv_dim)),
                  full((1, LANES)), full((N_SSM_HEADS, 1)), full((1, LANES)), full((N_SSM_HEADS, 1)),
                  full((1, D_INNER)), full((1, D_INNER)), full((LANES, D_INNER))],
        out_specs=pl.BlockSpec((L, D_INNER), lambda bi, ci: (row(bi, ci), 0)),
        out_shape=jax.ShapeDtypeStruct((b * s, D_INNER), BF16),
        scratch_shapes=[pltpu.VMEM((L + 8, conv_dim), F32), pltpu.VMEM((L, conv_dim), F32),
                        pltpu.VMEM((N_SSM_GROUPS, D_STATE, D_INNER // N_SSM_GROUPS), F32),
                        pltpu.VMEM((L, D_INNER), F32)],
        compiler_params=_cparams(("parallel", "arbitrary")),
        name="ssd",
    )(proj, proj, proj, proj, small, dtt, p["conv_w"], p["conv_b"], p["dt_b"], p["dt_bt"], p["a_log"], p["a_logt"],
      p["d_skip"], p["norm_g"], p["expand"])


def _sconv_kernel(b_ref, c_ref, x_ref, w_ref, o_ref, buf_ref):
    i = pl.program_id(1)
    tq = b_ref.shape[0]
    HALO = 8

    @pl.when(i == 0)
    def _():
        buf_ref[0:HALO, :] = jnp.zeros((HALO, buf_ref.shape[1]), F32)

    buf_ref[HALO:HALO + tq, :] = c_ref[...].astype(F32) * x_ref[...].astype(F32)
    v = jnp.zeros((tq, buf_ref.shape[1]), F32)
    for j in range(SC_CONV):
        r0 = HALO - (SC_CONV - 1) + j
        v = v + w_ref[j:j + 1, :] * buf_ref[r0:r0 + tq, :]
    o_ref[...] = (b_ref[...].astype(F32) * v).astype(BF16)
    buf_ref[0:HALO, :] = buf_ref[tq:tq + HALO, :]


def _sconv(proj, w, b, s):
    tq = min(512, s)
    nq = s // tq
    d = D_MODEL
    row = lambda bi, i: bi * nq + i
    return pl.pallas_call(
        _sconv_kernel,
        grid=(b, nq),
        in_specs=[pl.BlockSpec((tq, d), lambda bi, i: (row(bi, i), COL_SCB // d)),
                  pl.BlockSpec((tq, d), lambda bi, i: (row(bi, i), COL_SCC // d)),
                  pl.BlockSpec((tq, d), lambda bi, i: (row(bi, i), COL_SCX // d)),
                  pl.BlockSpec((SC_CONV, d), lambda bi, i: (0, 0))],
        out_specs=pl.BlockSpec((tq, d), lambda bi, i: (row(bi, i), 0)),
        out_shape=jax.ShapeDtypeStruct((b * s, d), BF16),
        scratch_shapes=[pltpu.VMEM((tq + 8, d), F32)],
        compiler_params=_cparams(("parallel", "arbitrary")),
        name="sconv",
    )(proj, proj, proj, w)


def _dsa_prep_kernel(ckv_ref, cq_ref, small_ref, qg_ref, kvg_ref, kg_ref, wqi_ref, wuq_ref, wuk_ref,
                     qi_ref, ql_ref, ckvn_ref, kidx_ref):
    def norm(v, g):
        return v * lax.rsqrt(jnp.mean(v * v, axis=-1, keepdims=True) + EPS) * g

    cq = norm(cq_ref[...].astype(F32), qg_ref[...]).astype(BF16)
    ckvn_ref[...] = norm(ckv_ref[...].astype(F32), kvg_ref[...]).astype(ATT_MM_DTYPE)
    sm = small_ref[...]
    lane = lax.broadcasted_iota(I32, sm.shape, 1)
    ksq = jnp.where(lane >= SM_KIDX, sm * sm, 0.0)
    kidx_ref[...] = (sm * lax.rsqrt(jnp.sum(ksq, axis=-1, keepdims=True) * (1.0 / D_IDX) + EPS) * kg_ref[...]).astype(BF16)
    qi = jnp.dot(cq, wqi_ref[...], preferred_element_type=F32)
    for h in range(N_IDX_HEADS):
        qi_ref[0, h] = qi[:, h * LANES:(h + 1) * LANES].astype(BF16)
    q = jnp.dot(cq, wuq_ref[...], preferred_element_type=F32).astype(BF16)
    for h in range(N_ATT_HEADS):
        ql = jnp.dot(q[:, h * LANES:(h + 1) * LANES], wuk_ref[h], preferred_element_type=F32)
        ql_ref[0, h] = (ql * (ATT_SCALE * LOG2_E)).astype(ATT_MM_DTYPE)


def _dsa_prep(proj, small, p, b, s):
    tm = min(512, s)
    nq = s // tm
    row = lambda bi, i: bi * nq + i
    full = lambda shape: pl.BlockSpec(shape, lambda bi, i: tuple(0 for _ in shape))
    return pl.pallas_call(
        _dsa_prep_kernel,
        grid=(b, nq),
        in_specs=[pl.BlockSpec((tm, KV_LORA), lambda bi, i: (row(bi, i), COL_CKV // KV_LORA)),
                  pl.BlockSpec((tm, Q_LORA), lambda bi, i: (row(bi, i), COL_CQ // Q_LORA)),
                  pl.BlockSpec((tm, LANES), lambda bi, i: (row(bi, i), 0)),
                  full((1, Q_LORA)), full((1, KV_LORA)), full((1, LANES)),
                  full((Q_LORA, N_IDX_HEADS * LANES)), full((Q_LORA, N_ATT_HEADS * LANES)),
                  full((N_ATT_HEADS, LANES, KV_LORA))],
        out_specs=[pl.BlockSpec((1, N_IDX_HEADS, tm, LANES), lambda bi, i: (bi, 0, i, 0)),
                   pl.BlockSpec((1, N_ATT_HEADS, tm, KV_LORA), lambda bi, i: (bi, 0, i, 0)),
                   pl.BlockSpec((tm, KV_LORA), lambda bi, i: (row(bi, i), 0)),
                   pl.BlockSpec((tm, LANES), lambda bi, i: (row(bi, i), 0))],
        out_shape=[jax.ShapeDtypeStruct((b, N_IDX_HEADS, s, LANES), BF16),
                   jax.ShapeDtypeStruct((b, N_ATT_HEADS, s, KV_LORA), ATT_MM_DTYPE),
                   jax.ShapeDtypeStruct((b * s, KV_LORA), ATT_MM_DTYPE),
                   jax.ShapeDtypeStruct((b * s, LANES), BF16)],
        compiler_params=_cparams(("parallel", "parallel")),
        name="dsa_prep",
    )(proj, proj, small, p["q_norm_g"], p["kv_norm_g"], p["k_norm_g"], p["w_qidx"], p["w_uq"], p["w_uk"])


def _for_chunk_groups(n, group, fn):
    def body(i, carry):
        fn([group * i + j for j in range(group)])
        return carry

    lax.fori_loop(0, n // group, body, 0)
    rest = n % group
    piece = group // 2
    while piece >= 1:
        start = n - rest + (rest // (2 * piece)) * (2 * piece)

        @pl.when((rest // piece) % 2 == 1)
        def _(start=start, piece=piece):
            fn([start + j for j in range(piece)])

        piece //= 2


def _dsa_kernel(qi_ref, ql_ref, small_ref, kidx_ref, ckv_ref, wuv_ref, o_ref,
                key_ref, whb_ref, d_ref, s_ref, sb_ref, p_ref, bias_ref,
                m_ref, l_ref, rmax_ref, acc_ref, cand_ref, *, topk, seq_bits):
    qb = pl.program_id(1)
    QB = Q_BLOCK
    TK = key_ref.shape[2]
    NT = TK // LANES
    nkc = (qb * QB + QB + TK - 1) // TK
    q_pos = qb * QB + lax.broadcasted_iota(I32, (QB, 1), 0)
    lane_tk = lax.broadcasted_iota(I32, (QB, TK), 1)

    sm = small_ref[...]
    for h in range(N_IDX_HEADS):
        whb_ref[h] = jnp.broadcast_to(sm[:, SM_WIDX + h:SM_WIDX + h + 1] * IDX_SCALE, (QB, LANES))
    qi = qi_ref[0].reshape(N_IDX_HEADS * QB, LANES)

    def score_chunks(kcs):
        offs = [pl.multiple_of(kc * TK, TK) for kc in kcs]
        for j in range(len(kcs)):
            kx = kidx_ref[pl.ds(offs[j], TK), :]
            d_ref[j] = lax.dot_general(qi, kx, _NT, preferred_element_type=F32)
        for j, kc in enumerate(kcs):
            for t in range(NT):
                sc = jnp.zeros((QB, LANES), F32)
                for h in range(N_IDX_HEADS):
                    d = d_ref[j, h * QB:(h + 1) * QB, t * LANES:(t + 1) * LANES]
                    sc = sc + jnp.maximum(d, 0.0) * whb_ref[h]
                bits = pltpu.bitcast(sc, I32)
                key = bits ^ ((bits >> 31) & 0x7FFFFFFF)
                kpos = offs[j] + t * LANES + lax.broadcasted_iota(I32, (QB, LANES), 1)
                key_ref[kc, :, t * LANES:(t + 1) * LANES] = jnp.where(kpos <= q_pos, key, KEY_MIN)

    _for_chunk_groups(nkc, SCORE_CHUNKS_PER_STEP, score_chunks)

    kk = jnp.minimum(topk, q_pos + 1).astype(F32)

    def count(pred):
        def body(kc, acc):
            k = key_ref[kc]
            for t in range(NT):
                acc = acc + jnp.where(pred(k[:, t * LANES:(t + 1) * LANES], kc * TK + t * LANES), 1.0, 0.0)
            return acc
        acc = lax.fori_loop(0, nkc, body, jnp.zeros((QB, LANES), F32))
        return jnp.sum(acc, axis=-1, keepdims=True)

    def value_bit(i, u):
        cand_u = u | jnp.left_shift(jnp.int32(1), 31 - i)
        cand = cand_u ^ KEY_MIN
        cnt = count(lambda k, base: k >= cand)
        return jnp.where(cnt >= kk, cand_u, u)

    def full_select():
        t = lax.fori_loop(0, 32, value_bit, jnp.zeros((QB, 1), I32)) ^ KEY_MIN
        return t, count(lambda k, base: k > t), count(lambda k, base: k >= t)

    RG = CAND_ROWS

    def collect(rg, carry):
        r0 = pl.multiple_of(rg * RG, RG)

        def insert(kc, tops):
            tops = list(tops)
            for t in range(NT):
                x = key_ref[kc, pl.ds(r0, RG), t * LANES:(t + 1) * LANES]
                for i in range(TOP_PER_LANE):
                    keep = tops[i] >= x
                    tops[i], x = jnp.where(keep, tops[i], x), jnp.where(keep, x, tops[i])
            return tuple(tops)

        tops = lax.fori_loop(0, nkc, insert, tuple(jnp.full((RG, LANES), KEY_MIN, I32) for _ in range(TOP_PER_LANE)))
        for i in range(TOP_PER_LANE):
            cand_ref[i, pl.ds(r0, RG), :] = tops[i]
        return carry

    lax.fori_loop(0, QB // RG, collect, 0)

    def count_cand(pred):
        acc = jnp.zeros((QB, LANES), F32)
        for i in range(TOP_PER_LANE):
            acc = acc + jnp.where(pred(cand_ref[i]), 1.0, 0.0)
        return jnp.sum(acc, axis=-1, keepdims=True)

    def cand_bit(i, u):
        cand_u = u | jnp.left_shift(jnp.int32(1), 31 - i)
        cand = cand_u ^ KEY_MIN
        return jnp.where(count_cand(lambda k: k >= cand) >= kk, cand_u, u)

    thr_c = lax.fori_loop(0, 32, cand_bit, jnp.zeros((QB, 1), I32)) ^ KEY_MIN
    lane_full = jnp.max(jnp.where(cand_ref[TOP_PER_LANE - 1] >= thr_c, 1.0, 0.0)) > 0.0
    thr, c_gt, c_ge = lax.cond(
        lane_full, full_select,
        lambda: (thr_c, count_cand(lambda k: k > thr_c), count_cand(lambda k: k >= thr_c)))
    need = kk - c_gt
    lane128 = lax.broadcasted_iota(I32, (QB, LANES), 1)

    def tie_cut():
        def index_bit(i, pcut):
            cand = pcut | jnp.left_shift(jnp.int32(1), seq_bits - 1 - i)
            cnt = count(lambda k, base: (k == thr) & (base + lane128 < cand))
            return jnp.where(cnt < need, cand, pcut)
        return lax.fori_loop(0, seq_bits, index_bit, jnp.zeros((QB, 1), I32))

    has_ties = jnp.max(c_ge - kk) > 0.0
    pcut = lax.cond(has_ties, tie_cut, lambda: jnp.full((QB, 1), 2 ** seq_bits - 1, I32))

    m_ref[...] = jnp.full(m_ref.shape, NEG_BIG, F32)
    l_ref[...] = jnp.zeros(l_ref.shape, F32)
    acc_ref[...] = jnp.zeros(acc_ref.shape, F32)
    HQ = ATT_QUARTER_HEADS
    n_quarters = N_ATT_HEADS // HQ
    RB = ATT_ROW_BLOCK
    n_rb = QB // RB

    def attend(kcs):
        offs = [pl.multiple_of(kc * TK, TK) for kc in kcs]
        for j, kc in enumerate(kcs):
            k = key_ref[kc]
            sel = (k > thr) | ((k == thr) & (offs[j] + lane_tk <= pcut))
            bias_ref[j] = jnp.where(sel, 0.0, NEG_BIG).astype(BF16)

        def logits(j, qt):
            ql = ql_ref[0, qt * HQ:(qt + 1) * HQ].reshape(HQ * QB, KV_LORA)
            kv = ckv_ref[pl.ds(offs[j], TK), :]
            s_ref[qt] = lax.dot_general(ql, kv, _NT, preferred_element_type=F32)

        def row_max(j, qt, rb):
            bias = bias_ref[j, rb * RB:(rb + 1) * RB, :]
            for hh in range(HQ):
                r0 = hh * QB + rb * RB
                s = s_ref[qt, r0:r0 + RB, :].astype(BF16) + bias
                sb_ref[qt, r0:r0 + RB, :] = s
                mx = s[:, 0:LANES]
                for t in range(1, NT):
                    mx = jnp.maximum(mx, s[:, t * LANES:(t + 1) * LANES])
                g0 = qt * HQ * QB + r0
                rmax_ref[g0:g0 + RB, :] = jnp.broadcast_to(
                    jnp.max(mx.astype(F32), axis=-1, keepdims=True), (RB, LANES))

        def row_probs(qt, rb):
            for hh in range(HQ):
                r0 = hh * QB + rb * RB
                g0 = qt * HQ * QB + r0
                m_prev = m_ref[g0:g0 + RB, :]
                m_new = jnp.maximum(m_prev, rmax_ref[g0:g0 + RB, :])
                alpha = jnp.exp2(m_prev - m_new)
                m_b = m_new.astype(BF16)
                psum = None
                for t in range(NT):
                    cols = slice(t * LANES, (t + 1) * LANES)
                    p = jnp.exp2(sb_ref[qt, r0:r0 + RB, cols] - m_b)
                    psum = p if psum is None else psum + p
                    p_ref[qt, r0:r0 + RB, cols] = p.astype(ATT_MM_DTYPE)
                l_ref[g0:g0 + RB, :] = alpha * l_ref[g0:g0 + RB, :] + psum.astype(F32)
                m_ref[g0:g0 + RB, :] = m_new
                for c in range(KV_LORA // LANES):
                    cols = slice(c * LANES, (c + 1) * LANES)
                    acc_ref[g0:g0 + RB, cols] = acc_ref[g0:g0 + RB, cols] * alpha

        items = [(j, qt) for j in range(len(kcs)) for qt in range(n_quarters)]
        logits(*items[0])
        for i, (j, qt) in enumerate(items):
            if i + 1 < len(items):
                logits(*items[i + 1])
            for rb in range(n_rb):
                row_max(j, qt, rb)
            for rb in range(n_rb):
                row_probs(qt, rb)
            rows = slice(qt * HQ * QB, (qt + 1) * HQ * QB)
            kv = ckv_ref[pl.ds(offs[j], TK), :]
            acc_ref[rows, :] += jnp.dot(p_ref[qt], kv, preferred_element_type=F32)

    _for_chunk_groups(nkc, ATT_CHUNKS_PER_STEP, attend)

    for pr in range(N_ATT_HEADS // 2):
        out = jnp.zeros((QB, LANES), F32)
        for half in range(2):
            h = 2 * pr + half
            rows = slice(h * QB, (h + 1) * QB)
            l_row = jnp.sum(l_ref[rows, :], axis=-1, keepdims=True)
            o_lat = (acc_ref[rows, :] / l_row).astype(BF16)
            out = out + jnp.dot(o_lat, wuv_ref[h], preferred_element_type=F32)
        o_ref[:, pr * LANES:(pr + 1) * LANES] = out.astype(BF16)


def _dsa(qi, ql, small, kidx, ckvn, wuv, b, s):
    QB = Q_BLOCK
    TK = min(DSA_TK, s)
    nq = s // QB
    topk = min(TOPK_MAX, s // 4)
    seq_bits = (s - 1).bit_length()
    resident = dict(pipeline_mode=pl.Buffered(1))
    return pl.pallas_call(
        functools.partial(_dsa_kernel, topk=topk, seq_bits=seq_bits),
        grid=(b, nq),
        in_specs=[pl.BlockSpec((1, N_IDX_HEADS, QB, LANES), lambda bi, i: (bi, 0, i, 0)),
                  pl.BlockSpec((1, N_ATT_HEADS, QB, KV_LORA), lambda bi, i: (bi, 0, i, 0)),
                  pl.BlockSpec((QB, LANES), lambda bi, i: (bi * nq + i, 0)),
                  pl.BlockSpec((s, LANES), lambda bi, i: (bi, 0), **resident),
                  pl.BlockSpec((s, KV_LORA), lambda bi, i: (bi, 0), **resident),
                  pl.BlockSpec((N_ATT_HEADS, KV_LORA, LANES), lambda bi, i: (0, 0, 0))],
        out_specs=pl.BlockSpec((QB, N_ATT_HEADS * ATT_V_DIM), lambda bi, i: (bi * nq + i, 0)),
        out_shape=jax.ShapeDtypeStruct((b * s, N_ATT_HEADS * ATT_V_DIM), BF16),
        scratch_shapes=[pltpu.VMEM((s // TK, QB, TK), I32),
                        pltpu.VMEM((N_IDX_HEADS, QB, LANES), F32),
                        pltpu.VMEM((SCORE_CHUNKS_PER_STEP, N_IDX_HEADS * QB, TK), F32),
                        pltpu.VMEM((N_ATT_HEADS // ATT_QUARTER_HEADS, ATT_QUARTER_HEADS * QB, TK), F32),
                        pltpu.VMEM((N_ATT_HEADS // ATT_QUARTER_HEADS, ATT_QUARTER_HEADS * QB, TK), BF16),
                        pltpu.VMEM((N_ATT_HEADS // ATT_QUARTER_HEADS, ATT_QUARTER_HEADS * QB, TK), ATT_MM_DTYPE),
                        pltpu.VMEM((ATT_CHUNKS_PER_STEP, QB, TK), BF16),
                        pltpu.VMEM((N_ATT_HEADS * QB, LANES), F32),
                        pltpu.VMEM((N_ATT_HEADS * QB, LANES), F32),
                        pltpu.VMEM((N_ATT_HEADS * QB, LANES), F32),
                        pltpu.VMEM((N_ATT_HEADS * QB, KV_LORA), F32),
                        pltpu.VMEM((TOP_PER_LANE, QB, LANES), I32)],
        compiler_params=_cparams(("parallel", "arbitrary")),
        name="dsa",
    )(qi, ql, small, kidx, ckvn, wuv)


def _merge_kernel(x_ref, ya_ref, yb_ref, yc_ref, ga_ref, gb_ref, gc_ref, wa_ref, wb_ref, wc_ref, wm_ref, o_ref):
    def branch(y_ref, w_ref, g_ref):
        y = jnp.dot(y_ref[...], w_ref[...], preferred_element_type=F32)
        return jax.nn.sigmoid(g_ref[...].astype(F32)) * y

    merged = branch(ya_ref, wa_ref, ga_ref) + branch(yb_ref, wb_ref, gb_ref) + branch(yc_ref, wc_ref, gc_ref)
    o_ref[...] = x_ref[...] + jnp.dot(merged.astype(BF16), wm_ref[...], preferred_element_type=F32)


def _merge(x2, ya, yb, yc, proj, p):
    t, d = x2.shape
    tm = min(512, t)
    tok = lambda cols, cb: pl.BlockSpec((tm, cols), lambda i: (i, cb))
    full = lambda shape: pl.BlockSpec(shape, lambda i: (0, 0))
    return pl.pallas_call(
        _merge_kernel,
        grid=(t // tm,),
        in_specs=[tok(d, 0), tok(D_INNER, 0), tok(d, 0), tok(d, 0),
                  tok(d, COL_GA // d), tok(d, COL_GB // d), tok(d, COL_GC // d),
                  full((D_INNER, d)), full((d, d)), full((d, d)), full((d, d))],
        out_specs=tok(d, 0),
        out_shape=jax.ShapeDtypeStruct((t, d), F32),
        compiler_params=_cparams(("parallel",)),
        name="merge",
    )(x2, ya, yb, yc, proj, proj, proj, p["ssm_w_out"], p["sc_w_out"], p["att_w_out"], p["mix_w_out"])


def _first_argmax(v, lane):
    m = jnp.max(v, axis=-1, keepdims=True)
    idx = jnp.min(jnp.where(v == m, lane, LANES), axis=-1, keepdims=True)
    return m, idx


def _moe_kernel(x_ref, g_ref, wr_ref, br_ref, wg_ref, wu_ref, wd_ref, fg_ref, o_ref, h_ref, comb_ref, acc_ref,
                *, final_norm):
    e = pl.program_id(1)

    @pl.when(e == 0)
    def _():
        x = x_ref[...]
        h = x * lax.rsqrt(jnp.mean(x * x, axis=-1, keepdims=True) + EPS) * g_ref[...]
        h_ref[...] = h.astype(BF16)
        logits = jnp.dot(h, wr_ref[...], precision=lax.Precision.HIGHEST, preferred_element_type=F32) + br_ref[...]
        lane = lax.broadcasted_iota(I32, logits.shape, 1)
        neg_inf = -jnp.inf
        lg = jnp.where(lane < N_EXPERT_GROUPS, logits, neg_inf)
        eg = jnp.exp(lg - jnp.max(lg, axis=-1, keepdims=True))
        pg = eg / jnp.sum(eg, axis=-1, keepdims=True)
        p_group, g_sel = _first_argmax(jnp.where(lane < N_EXPERT_GROUPS, pg, neg_inf), lane)
        lo = N_EXPERT_GROUPS + g_sel * EXPERTS_PER_GROUP
        in_group = (lane >= lo) & (lane < lo + EXPERTS_PER_GROUP)
        le = jnp.where(in_group, logits, neg_inf)
        ee = jnp.exp(le - jnp.max(le, axis=-1, keepdims=True))
        pe = jnp.where(in_group, ee / jnp.sum(ee, axis=-1, keepdims=True), neg_inf)
        p1, i1 = _first_argmax(pe, lane)
        p2, i2 = _first_argmax(jnp.where(lane == i1, neg_inf, pe), lane)
        tot = p1 + p2
        comb_ref[...] = (jnp.where(lane == i1, p_group * (p1 / tot), 0.0)
                         + jnp.where(lane == i2, p_group * (p2 / tot), 0.0))
        acc_ref[...] = jnp.zeros(acc_ref.shape, F32)

    h = h_ref[...]
    lane = lax.broadcasted_iota(I32, comb_ref.shape, 1)
    c_e = jnp.sum(jnp.where(lane == e + N_EXPERT_GROUPS, comb_ref[...], 0.0), axis=-1, keepdims=True)
    gate = jnp.dot(h, wg_ref[0], preferred_element_type=F32)
    up = jnp.dot(h, wu_ref[0], preferred_element_type=F32)
    hid = (_silu(gate) * up * c_e).astype(BF16)
    acc_ref[...] += jnp.dot(hid, wd_ref[0], preferred_element_type=F32)

    @pl.when(e == N_EXPERTS - 1)
    def _():
        y = x_ref[...] + acc_ref[...]
        if final_norm:
            y = y * lax.rsqrt(jnp.mean(y * y, axis=-1, keepdims=True) + EPS) * fg_ref[...]
        o_ref[...] = y


def _moe(x2, p, final_g, final_norm):
    t, d = x2.shape
    tm = min(1024, t)
    full = lambda shape: pl.BlockSpec(shape, lambda i, e: (0, 0))
    return pl.pallas_call(
        functools.partial(_moe_kernel, final_norm=final_norm),
        grid=(t // tm, N_EXPERTS),
        in_specs=[pl.BlockSpec((tm, d), lambda i, e: (i, 0)), full((1, d)), full((d, LANES)), full((1, LANES)),
                  pl.BlockSpec((1, d, D_EXPERT), lambda i, e: (e, 0, 0)),
                  pl.BlockSpec((1, d, D_EXPERT), lambda i, e: (e, 0, 0)),
                  pl.BlockSpec((1, D_EXPERT, d), lambda i, e: (e, 0, 0)),
                  full((1, d))],
        out_specs=pl.BlockSpec((tm, d), lambda i, e: (i, 0)),
        out_shape=jax.ShapeDtypeStruct((t, d), F32),
        scratch_shapes=[pltpu.VMEM((tm, d), BF16), pltpu.VMEM((tm, LANES), F32), pltpu.VMEM((tm, d), F32)],
        compiler_params=_cparams(("parallel", "arbitrary")),
        name="moe",
    )(x2, p["norm2_g"], p["w_router"], p["b_router"], p["w_gate"], p["w_up"], p["w_down"], final_g)


def _pad_cols(w, width, offset, total):
    return jnp.pad(w, [(0, 0)] * (w.ndim - 1) + [(offset, total - offset - width)])


def _layer_params(l, norm1_g, w_in, ssm_conv_w, ssm_conv_b, ssm_dt_bias, ssm_a_log, ssm_d, ssm_norm_g, ssm_w_out,
                  sc_conv_w, sc_w_out, att_q_norm_g, att_kv_norm_g, idx_k_norm_g, att_w_uq, idx_w_q, att_w_uk,
                  att_w_uv, att_w_out, mix_w_out, norm2_g, moe_w_group, moe_b_group, moe_w_expert, moe_b_expert,
                  moe_w_gate, moe_w_up, moe_w_down):
    w = w_in[l]
    o = 0
    seg = {}
    for name, size in (("z", 2048), ("xbc", 3072), ("dt", 32), ("scb", 1024), ("scc", 1024), ("scx", 1024),
                       ("cq", Q_LORA), ("ckv", KV_LORA), ("kidx", D_IDX), ("widx", N_IDX_HEADS),
                       ("ga", 1024), ("gb", 1024), ("gc", 1024)):
        seg[name] = w[:, o:o + size]
        o += size
    d = w.shape[0]
    small = jnp.concatenate([seg["dt"], seg["widx"], jnp.zeros((d, SM_KIDX - SM_WIDX - N_IDX_HEADS), F32),
                             seg["kidx"]], axis=1)
    w_perm = jnp.concatenate([seg["z"], seg["xbc"], seg["scb"], seg["scc"], seg["scx"], seg["ga"], seg["gb"],
                              seg["gc"], seg["ckv"], seg["cq"], small,
                              jnp.zeros((d, N_PROJ - COL_SMALL - LANES), F32)], axis=1).astype(BF16)
    head_of_lane = jnp.arange(LANES)[:, None]
    head_of_chan = (jnp.arange(D_INNER) // SSM_HEAD_DIM)[None, :]
    wqi = idx_w_q[l]
    wuq = att_w_uq[l]
    wuv = att_w_uv[l]
    wuv_pad = jnp.stack([_pad_cols(wuv[h], ATT_V_DIM, (h % 2) * ATT_V_DIM, LANES) for h in range(N_ATT_HEADS)])
    return {
        "norm1_g": norm1_g[l][None, :], "w_perm": w_perm,
        "conv_w": ssm_conv_w[l], "conv_b": ssm_conv_b[l][None, :],
        "dt_b": _pad_cols(ssm_dt_bias[l][None, :], N_SSM_HEADS, SM_DT, LANES), "dt_bt": ssm_dt_bias[l][:, None],
        "a_log": _pad_cols(ssm_a_log[l][None, :], N_SSM_HEADS, SM_DT, LANES), "a_logt": ssm_a_log[l][:, None],
        "d_skip": jnp.repeat(ssm_d[l], SSM_HEAD_DIM)[None, :], "norm_g": ssm_norm_g[l][None, :],
        "expand": (head_of_lane == head_of_chan).astype(BF16),
        "sc_conv_w": sc_conv_w[l],
        "q_norm_g": att_q_norm_g[l][None, :], "kv_norm_g": att_kv_norm_g[l][None, :],
        "k_norm_g": _pad_cols(idx_k_norm_g[l][None, :], D_IDX, SM_KIDX, LANES),
        "w_qidx": _pad_cols(wqi, D_IDX, SM_KIDX, LANES).reshape(Q_LORA, N_IDX_HEADS * LANES).astype(BF16),
        "w_uq": _pad_cols(wuq, ATT_HEAD_DIM, 0, LANES).reshape(Q_LORA, N_ATT_HEADS * LANES).astype(BF16),
        "w_uk": jnp.pad(att_w_uk[l], ((0, 0), (0, LANES - ATT_HEAD_DIM), (0, 0))).astype(BF16),
        "w_uv": wuv_pad.astype(BF16),
        "ssm_w_out": ssm_w_out[l].astype(BF16), "sc_w_out": sc_w_out[l].astype(BF16),
        "att_w_out": att_w_out[l].astype(BF16), "mix_w_out": mix_w_out[l].astype(BF16),
        "norm2_g": norm2_g[l][None, :],
        "w_router": _pad_cols(jnp.concatenate([moe_w_group[l], moe_w_expert[l]], axis=1),
                              N_EXPERT_GROUPS + N_EXPERTS, 0, LANES),
        "b_router": _pad_cols(jnp.concatenate([moe_b_group[l], moe_b_expert[l]])[None, :],
                              N_EXPERT_GROUPS + N_EXPERTS, 0, LANES),
        "w_gate": moe_w_gate[l].astype(BF16), "w_up": moe_w_up[l].astype(BF16), "w_down": moe_w_down[l].astype(BF16),
    }


def kernel(x, norm1_g, w_in, ssm_conv_w, ssm_conv_b, ssm_dt_bias, ssm_a_log, ssm_d, ssm_norm_g, ssm_w_out, sc_conv_w, sc_w_out, att_q_norm_g, att_kv_norm_g, idx_k_norm_g, att_w_uq, idx_w_q, att_w_uk, att_w_uv, att_w_out, mix_w_out, norm2_g, moe_w_group, moe_b_group, moe_w_expert, moe_b_expert, moe_w_gate, moe_w_up, moe_w_down, final_norm_g):
    b, s, d = x.shape
    depth = w_in.shape[0]
    assert d == D_MODEL and s % DSA_TK == 0 or s < DSA_TK
    x2 = x.reshape(b * s, d)
    final_g = final_norm_g[None, :]
    for l in range(depth):
        p = _layer_params(l, norm1_g, w_in, ssm_conv_w, ssm_conv_b, ssm_dt_bias, ssm_a_log, ssm_d, ssm_norm_g,
                          ssm_w_out, sc_conv_w, sc_w_out, att_q_norm_g, att_kv_norm_g, idx_k_norm_g, att_w_uq,
                          idx_w_q, att_w_uk, att_w_uv, att_w_out, mix_w_out, norm2_g, moe_w_group, moe_b_group,
                          moe_w_expert, moe_b_expert, moe_w_gate, moe_w_up, moe_w_down)
        proj, small = _in_proj(x2, p["norm1_g"], p["w_perm"])
        dtt = jnp.swapaxes(small[:, SM_DT:SM_DT + N_SSM_HEADS].reshape(b, s, N_SSM_HEADS), 1, 2)
        ya = _ssd(proj, small, dtt, p, b, s)
        yb = _sconv(proj, p["sc_conv_w"], b, s)
        qi, ql, ckvn, kidx = _dsa_prep(proj, small, p, b, s)
        yc = _dsa(qi, ql, small, kidx, ckvn, p["w_uv"], b, s)
        x2 = _merge(x2, ya, yb, yc, proj, p)
        x2 = _moe(x2, p, final_g, final_norm=(l == depth - 1))
    return x2.reshape(b, s, d)
```
